```python
import math
import jax
import jax.numpy as jnp
from jax import lax
import numpy as np

D_MODEL = 1024
BATCH = 4
SEQ = 8192
DEPTH = 2

N_A_LAYERS = DEPTH // 2
N_B_LAYERS = DEPTH - N_A_LAYERS
PLE_DIM = 256
D_FF = 2816
EPS = 1e-6
N_SUBNORMS = 8

MLSTM_HEADS = 4
MLSTM_DV = D_MODEL // MLSTM_HEADS
MLSTM_DK = MLSTM_DV // 2
MLSTM_CHUNK = 128
MLSTM_QK_W = MLSTM_HEADS * MLSTM_DK
MLSTM_IN = 2 * MLSTM_QK_W + 2 * D_MODEL + 2 * MLSTM_HEADS

DIFF_HEAD_DIM = 64
DIFF_HEADS = D_MODEL // (2 * DIFF_HEAD_DIM)
Q_BLOCK = 128

kernel_name = "yoco_mlstm_diffattn_macaron_block"


def rms_norm(x, g):
    xf = x.astype(jnp.float32)
    y = xf * lax.rsqrt(jnp.mean(xf * xf, axis=-1, keepdims=True) + EPS)
    return (y * g.astype(jnp.float32)).astype(x.dtype)


def swiglu(x, w_in, w_out):
    gate, up = jnp.split(x @ w_in, 2, axis=-1)
    return (jax.nn.silu(gate) * up) @ w_out


def mlstm_chunkwise(q, k, v, log_i, log_f):
    B, H, S, DK = q.shape
    DV = v.shape[-1]
    L = MLSTM_CHUNK
    NC = S // L
    q, k, v = (t.astype(jnp.float32) for t in (q, k, v))

    def chunks(t):
        return jnp.moveaxis(t.reshape(B, H, NC, L, *t.shape[3:]), 2, 0)

    xs = tuple(chunks(t) for t in (q, k, v, log_i, log_f))
    causal = jnp.tril(jnp.ones((L, L), dtype=bool))

    def step(carry, inp):
        C, n, m = carry
        qj, kj, vj, li, lf = inp
        b = jnp.cumsum(lf, axis=-1)
        d = b[..., :, None] - b[..., None, :] + li[..., None, :]
        d = jnp.where(causal, d, -jnp.inf)
        inter = b + m[..., None]
        m_t = jnp.maximum(inter, jnp.max(d, axis=-1))
        w = jnp.exp(d - m_t[..., None])
        s_inter = jnp.exp(inter - m_t)
        qk = jnp.einsum('bhtd,bhsd->bhts', qj, kj) * w
        num = (jnp.einsum('bhts,bhsv->bhtv', qk, vj)
               + s_inter[..., None] * jnp.einsum('bhtd,bhdv->bhtv', qj, C))
        den = jnp.sum(qk, axis=-1) + s_inter * jnp.einsum('bhtd,bhd->bht', qj, n)
        h = num / jnp.maximum(jnp.abs(den), jnp.exp(-m_t))[..., None]
        bL = b[..., -1]
        g = bL[..., None] - b + li
        m_new = jnp.maximum(bL + m, jnp.max(g, axis=-1))
        sc = jnp.exp(g - m_new[..., None])
        decay = jnp.exp(bL + m - m_new)
        C_new = decay[..., None, None] * C + jnp.einsum('bhs,bhsd,bhsv->bhdv', sc, kj, vj)
        n_new = decay[..., None] * n + jnp.einsum('bhs,bhsd->bhd', sc, kj)
        return (C_new, n_new, m_new), h

    init = (jnp.zeros((B, H, DK, DV), jnp.float32),
            jnp.zeros((B, H, DK), jnp.float32),
            jnp.full((B, H), -jnp.inf, jnp.float32))
    _, hs = lax.scan(step, init, xs)
    return jnp.moveaxis(hs, 0, 2).reshape(B, H, S, DV)


def mlstm_mixer(x, w_in, b_gates, head_norm, w_out):
    B, S, _ = x.shape
    proj = x @ w_in
    q, k, v, o, gates = jnp.split(
        proj, [MLSTM_QK_W, 2 * MLSTM_QK_W, 2 * MLSTM_QK_W + D_MODEL,
               2 * MLSTM_QK_W + 2 * D_MODEL], axis=-1)

    def heads(t, d):
        return t.reshape(B, S, MLSTM_HEADS, d).transpose(0, 2, 1, 3)

    q = heads(q, MLSTM_DK)
    k = heads(k, MLSTM_DK) * (MLSTM_DK ** -0.5)
    v = heads(v, MLSTM_DV)
    gates = gates.astype(jnp.float32) + b_gates.astype(jnp.float32)
    log_i = gates[..., :MLSTM_HEADS].transpose(0, 2, 1)
    log_f = jax.nn.log_sigmoid(gates[..., MLSTM_HEADS:]).transpose(0, 2, 1)
    h = mlstm_chunkwise(q, k, v, log_i, log_f)
    h = rms_norm(h, head_norm[:, None, :])
    h = h.transpose(0, 2, 1, 3).reshape(B, S, D_MODEL).astype(x.dtype)
    return (jax.nn.sigmoid(o) * h) @ w_out


def shared_kv(x, kv_norm, w_kv):
    B, S, _ = x.shape
    kv = rms_norm(x, kv_norm) @ w_kv
    k, v = jnp.split(kv, 2, axis=-1)
    k = k.reshape(B, S, 2 * DIFF_HEADS, DIFF_HEAD_DIM).transpose(0, 2, 1, 3)
    v = v.reshape(B, S, DIFF_HEADS, 2 * DIFF_HEAD_DIM).transpose(0, 2, 1, 3)
    return k, v


def diff_attention(x, k_sh, v_sh, w_q, lam_vecs, subln, w_out, lam_init):
    B, S, _ = x.shape
    H, DH = DIFF_HEADS, DIFF_HEAD_DIM
    q = (x @ w_q).reshape(B, S, 2 * H, DH).transpose(0, 2, 1, 3) * (DH ** -0.5)
    lv = lam_vecs.astype(jnp.float32)
    lam = jnp.exp(jnp.sum(lv[0] * lv[1])) - jnp.exp(jnp.sum(lv[2] * lv[3])) + lam_init
    nb = S // Q_BLOCK
    qb = jnp.moveaxis(q.reshape(B, 2 * H, nb, Q_BLOCK, DH), 2, 0)
    key_pos = jnp.arange(S)

    def block(args):
        qi, start = args
        s = jnp.einsum('bhqd,bhkd->bhqk', qi, k_sh).astype(jnp.float32)
        qpos = start + jnp.arange(Q_BLOCK)
        s = jnp.where(key_pos[None, :] <= qpos[:, None], s, -jnp.inf)
        a = jax.nn.softmax(s, axis=-1).reshape(B, H, 2, Q_BLOCK, S)
        diff = a[:, :, 0] - lam * a[:, :, 1]
        return jnp.einsum('bhqk,bhkv->bhqv', diff.astype(v_sh.dtype), v_sh)

    o = lax.map(block, (qb, jnp.arange(nb) * Q_BLOCK))
    o = o.transpose(1, 2, 0, 3, 4).reshape(B, H, S, 2 * DH)
    o = rms_norm(o, subln) * (1.0 - lam_init)
    o = o.transpose(0, 2, 1, 3).reshape(B, S, D_MODEL).astype(x.dtype)
    return o @ w_out


def setup_inputs(seed: int = 0) -> dict:
    key = jax.random.key(seed)
    ks = jax.random.split(key, 20)
    f32 = jnp.float32

    def nrm(k, shape, fan_in):
        return jax.random.normal(k, shape, f32) * (fan_in ** -0.5)

    def gain(k, shape):
        return 1.0 + 0.05 * jax.random.normal(k, shape, f32)

    f_bias = jnp.linspace(3.0, 6.0, MLSTM_HEADS, dtype=f32)
    b_gates = jnp.concatenate(
        [0.1 * jax.random.normal(ks[8], (N_A_LAYERS, MLSTM_HEADS), f32),
         f_bias[None, :] + 0.1 * jax.random.normal(ks[9], (N_A_LAYERS, MLSTM_HEADS), f32)], axis=-1)
    return {
        "x": jax.random.normal(ks[0], (BATCH, SEQ, D_MODEL), f32),
        "p": jax.random.normal(ks[1], (DEPTH, BATCH, SEQ, PLE_DIM), f32),
        "norm_g": gain(ks[2], (DEPTH, N_SUBNORMS, D_MODEL)),
        "w_ffn_in": nrm(ks[3], (DEPTH, 2, D_MODEL, 2 * D_FF), D_MODEL),
        "w_ffn_out": nrm(ks[4], (DEPTH, 2, D_FF, D_MODEL), D_FF),
        "w_ple_proj": nrm(ks[5], (DEPTH, PLE_DIM, D_MODEL), PLE_DIM),
        "w_ple_gate": nrm(ks[6], (DEPTH, D_MODEL, D_MODEL), D_MODEL),
        "mlstm_w_in": nrm(ks[7], (N_A_LAYERS, D_MODEL, MLSTM_IN), D_MODEL),
        "mlstm_b_gates": b_gates,
        "mlstm_head_norm": gain(ks[10], (N_A_LAYERS, MLSTM_HEADS, MLSTM_DV)),
        "mlstm_w_out": nrm(ks[11], (N_A_LAYERS, D_MODEL, D_MODEL), D_MODEL),
        "kv_norm": gain(ks[12], (D_MODEL,)),
        "w_kv": nrm(ks[13], (D_MODEL, 2 * D_MODEL), D_MODEL),
        "diff_w_q": nrm(ks[14], (N_B_LAYERS, D_MODEL, D_MODEL), D_MODEL),
        "diff_lambda": 0.1 * jax.random.normal(ks[15], (N_B_LAYERS, 4, DIFF_HEAD_DIM), f32),
        "diff_subln": gain(ks[16], (N_B_LAYERS, 2 * DIFF_HEAD_DIM)),
        "diff_w_out": nrm(ks[17], (N_B_LAYERS, D_MODEL, D_MODEL), D_MODEL),
    }


def reference(x, p, norm_g, w_ffn_in, w_ffn_out, w_ple_proj, w_ple_gate,
              mlstm_w_in, mlstm_b_gates, mlstm_head_norm, mlstm_w_out,
              kv_norm, w_kv, diff_w_q, diff_lambda, diff_subln, diff_w_out):
    k_sh = None
    v_sh = None
    for layer in range(DEPTH):
        g = norm_g[layer]
        h = swiglu(rms_norm(x, g[0]), w_ffn_in[layer, 0], w_ffn_out[layer, 0])
        x = x + 0.5 * rms_norm(h, g[1])
        h = rms_norm(x, g[2])
        if layer < N_A_LAYERS:
            h = mlstm_mixer(h, mlstm_w_in[layer], mlstm_b_gates[layer],
                            mlstm_head_norm[layer], mlstm_w_out[layer])
        else:
            j = layer - N_A_LAYERS
            lam_init = 0.8 - 0.6 * math.exp(-0.3 * layer)
            h = diff_attention(h, k_sh, v_sh, diff_w_q[j], diff_lambda[j],
                               diff_subln[j], diff_w_out[j], lam_init)
        x = x + rms_norm(h, g[3])
        h = swiglu(rms_norm(x, g[4]), w_ffn_in[layer, 1], w_ffn_out[layer, 1])
        x = x + 0.5 * rms_norm(h, g[5])
        gate = jax.nn.sigmoid(rms_norm(x, g[6]) @ w_ple_gate[layer])
        e = p[layer].astype(x.dtype) @ w_ple_proj[layer]
        x = x + rms_norm(e * gate, g[7])
        if layer == N_A_LAYERS - 1:
            k_sh, v_sh = shared_kv(x, kv_norm, w_kv)
    return x
```

```python
import functools
import math

import jax
import jax.numpy as jnp
from jax import lax
from jax.experimental import pallas as pl
from jax.experimental.pallas import tpu as pltpu

F32 = jnp.float32
BF16 = jnp.bfloat16

D_MODEL = 1024
D_FF = 2816
PLE_DIM = 256
EPS = 1e-6

MLSTM_HEADS = 4
MLSTM_DV = 256
MLSTM_DK = 128
MLSTM_CHUNK = 128
MLSTM_QK_W = MLSTM_HEADS * MLSTM_DK
GATE_ROWS = 16

DIFF_HEAD_DIM = 64
DIFF_HEADS = 8
DIFF_PAIR_W = 2 * DIFF_HEAD_DIM

VMEM_LIMIT_BYTES = 56 * 1024 * 1024

TOKEN_TILE = 512
FFN_CHUNK = 256
MLSTM_SEQ_TILE = 1024
ATTN_Q_TILE = 512
ATTN_K_TILE = 512


def _rms(x, g):
    ms = jnp.mean(x * x, axis=-1, keepdims=True)
    return (x * lax.rsqrt(ms + EPS)) * g


def _dot(a, b):
    return jnp.dot(a, b, preferred_element_type=F32)


def _dot_nt(a, b):
    return lax.dot_general(a, b, (((1,), (1,)), ((), ())), preferred_element_type=F32)


def _params(*semantics):
    return pltpu.CompilerParams(dimension_semantics=semantics, vmem_limit_bytes=VMEM_LIMIT_BYTES)


def _resident(shape):
    nd = len(shape)
    return pl.BlockSpec(shape, lambda *_: (0,) * nd, pipeline_mode=pl.Buffered(1))


def _row_tile(width, tile=TOKEN_TILE):
    return pl.BlockSpec((tile, width), lambda i: (i, 0))


def _ffn_body(x_ref, g_ref, win_ref, wout_ref, o_ref, *, pre, post):
    x = x_ref[...]
    xn = _rms(x, g_ref[pre:pre + 1, :]).astype(BF16)
    acc = jnp.zeros(x.shape, F32)
    for c in range(D_FF // FFN_CHUNK):
        lo = c * FFN_CHUNK
        gate = _dot(xn, win_ref[:, lo:lo + FFN_CHUNK])
        up = _dot(xn, win_ref[:, D_FF + lo:D_FF + lo + FFN_CHUNK])
        act = (gate * jax.nn.sigmoid(gate) * up).astype(BF16)
        acc = acc + _dot(act, wout_ref[lo:lo + FFN_CHUNK, :])
    o_ref[...] = x + 0.5 * _rms(acc, g_ref[post:post + 1, :])


def _ffn(x, g, w_in, w_out, pre, post):
    t = x.shape[0]
    return pl.pallas_call(
        functools.partial(_ffn_body, pre=pre, post=post),
        grid=(t // TOKEN_TILE,),
        in_specs=[_row_tile(D_MODEL), _resident(g.shape), _resident(w_in.shape), _resident(w_out.shape)],
        out_specs=_row_tile(D_MODEL),
        out_shape=jax.ShapeDtypeStruct(x.shape, F32),
        compiler_params=_params("parallel"),
        name="ffn",
    )(x, g, w_in, w_out)


def _ple_body(x_ref, p_ref, g_ref, wgate_ref, wple_ref, *rest, with_kv):
    if with_kv:
        kvn_ref, wkv_ref, o_ref, kv_ref = rest
    else:
        (o_ref,) = rest
    x = x_ref[...]
    xn = _rms(x, g_ref[6:7, :]).astype(BF16)
    gate = jax.nn.sigmoid(_dot(xn, wgate_ref[...]))
    e = _dot(p_ref[...].astype(BF16), wple_ref[...])
    y = x + _rms(e * gate, g_ref[7:8, :])
    o_ref[...] = y
    if with_kv:
        yn = _rms(y, kvn_ref[...]).astype(BF16)
        kv_ref[...] = _dot(yn, wkv_ref[...]).astype(BF16)


def _ple(x, p, g, w_gate, w_ple, kv_norm=None, w_kv=None):
    t = x.shape[0]
    with_kv = w_kv is not None
    in_specs = [_row_tile(D_MODEL), _row_tile(PLE_DIM), _resident(g.shape),
                _resident(w_gate.shape), _resident(w_ple.shape)]
    args = [x, p, g, w_gate, w_ple]
    out_specs = _row_tile(D_MODEL)
    out_shape = jax.ShapeDtypeStruct(x.shape, F32)
    if with_kv:
        in_specs += [_resident(kv_norm.shape), _resident(w_kv.shape)]
        args += [kv_norm, w_kv]
        out_specs = [out_specs, _row_tile(2 * D_MODEL)]
        out_shape = [out_shape, jax.ShapeDtypeStruct((t, 2 * D_MODEL), BF16)]
    return pl.pallas_call(
        functools.partial(_ple_body, with_kv=with_kv),
        grid=(t // TOKEN_TILE,),
        in_specs=in_specs,
        out_specs=out_specs,
        out_shape=out_shape,
        compiler_params=_params("parallel"),
        name="ple_kv" if with_kv else "ple",
    )(*args)


def _mixer_out_body(x_ref, h_ref, g_ref, w_ref, o_ref):
    x = x_ref[...]
    y = _dot(h_ref[...], w_ref[...])
    o_ref[...] = x + _rms(y, g_ref[3:4, :])


def _mixer_out(x, h, g, w_out):
    t = x.shape[0]
    return pl.pallas_call(
        _mixer_out_body,
        grid=(t // TOKEN_TILE,),
        in_specs=[_row_tile(D_MODEL), _row_tile(D_MODEL), _resident(g.shape), _resident(w_out.shape)],
        out_specs=_row_tile(D_MODEL),
        out_shape=jax.ShapeDtypeStruct(x.shape, F32),
        compiler_params=_params("parallel"),
        name="mixer_out",
    )(x, h, g, w_out)


def _mlstm_proj_body(x_ref, g_ref, wq_ref, wkt_ref, wv_ref, wo_ref, wgt_ref, bg_ref,
                     q_ref, kt_ref, v_ref, o_ref, gates_ref):
    xn = _rms(x_ref[...], g_ref[2:3, :]).astype(BF16)
    q_ref[...] = _dot(xn, wq_ref[...]).astype(BF16)
    v_ref[...] = _dot(xn, wv_ref[...]).astype(BF16)
    o_ref[...] = _dot(xn, wo_ref[...])
    kt = (_dot_nt(wkt_ref[...], xn) * (MLSTM_DK ** -0.5)).astype(BF16)
    z = _dot_nt(wgt_ref[...], xn) + bg_ref[...]
    row = lax.broadcasted_iota(jnp.int32, z.shape, 0)
    is_forget = (row >= MLSTM_HEADS) & (row < 2 * MLSTM_HEADS)
    gates = jnp.where(is_forget, jax.nn.log_sigmoid(z), z)
    for c in range(TOKEN_TILE // MLSTM_CHUNK):
        sl = slice(c * MLSTM_CHUNK, (c + 1) * MLSTM_CHUNK)
        kt_ref[c] = kt[:, sl]
        gates_ref[c] = gates[:, sl]


def _mlstm_proj(x, g, wq, wkt, wv, wo, wgt, bg):
    t = x.shape[0]
    cpt = TOKEN_TILE // MLSTM_CHUNK
    nchunks = t // MLSTM_CHUNK
    return pl.pallas_call(
        _mlstm_proj_body,
        grid=(t // TOKEN_TILE,),
        in_specs=[_row_tile(D_MODEL), _resident(g.shape), _resident(wq.shape), _resident(wkt.shape),
                  _resident(wv.shape), _resident(wo.shape), _resident(wgt.shape), _resident(bg.shape)],
        out_specs=[
            _row_tile(MLSTM_QK_W),
            pl.BlockSpec((cpt, MLSTM_QK_W, MLSTM_CHUNK), lambda i: (i, 0, 0)),
            _row_tile(D_MODEL),
            _row_tile(D_MODEL),
            pl.BlockSpec((cpt, GATE_ROWS, MLSTM_CHUNK), lambda i: (i, 0, 0)),
        ],
        out_shape=[
            jax.ShapeDtypeStruct((t, MLSTM_QK_W), BF16),
            jax.ShapeDtypeStruct((nchunks, MLSTM_QK_W, MLSTM_CHUNK), BF16),
            jax.ShapeDtypeStruct((t, D_MODEL), BF16),
            jax.ShapeDtypeStruct((t, D_MODEL), F32),
            jax.ShapeDtypeStruct((nchunks, GATE_ROWS, MLSTM_CHUNK), F32),
        ],
        compiler_params=_params("parallel"),
        name="mlstm_proj",
    )(x, g, wq, wkt, wv, wo, wgt, bg)


def _split3(x):
    hi = x.astype(BF16)
    r = x - hi.astype(F32)
    mid = r.astype(BF16)
    lo = (r - mid.astype(F32)).astype(BF16)
    return hi, mid, lo


def _mlstm_core_body(q_ref, kt_ref, v_ref, o_ref, gates_ref, hn_ref, out_ref, c_ref, m_ref):
    L = MLSTM_CHUNK

    @pl.when(pl.program_id(1) == 0)
    def _():
        c_ref[...] = jnp.zeros(c_ref.shape, F32)
        m_ref[...] = jnp.full(m_ref.shape, -jnp.inf, F32)

    r_i = lax.broadcasted_iota(jnp.int32, (L, L), 0)
    c_i = lax.broadcasted_iota(jnp.int32, (L, L), 1)
    causal = c_i <= r_i
    upper = (r_i <= c_i).astype(BF16)
    ones_blk = jnp.ones((L, 128), BF16)

    def chunk(c, carry):
        row0 = pl.multiple_of(c * L, L)
        g_all = gates_ref[c]
        b_all = sum(_dot(part, upper) for part in _split3(g_all))
        for h in range(MLSTM_HEADS):
            li = g_all[h:h + 1, :]
            b_row = b_all[MLSTM_HEADS + h:MLSTM_HEADS + h + 1, :]
            m_prev = m_ref[h:h + 1, 0:1]
            q = q_ref[pl.ds(row0, L), h * MLSTM_DK:(h + 1) * MLSTM_DK]
            kt = kt_ref[c, h * MLSTM_DK:(h + 1) * MLSTM_DK, :]
            v = v_ref[pl.ds(row0, L), h * MLSTM_DV:(h + 1) * MLSTM_DV]
            v_ext = jnp.concatenate([v, ones_blk], axis=1)

            b_col = jnp.broadcast_to(b_row, (L, L)).T
            a_row = li - b_row
            d = jnp.where(causal, b_col + a_row, -jnp.inf)
            inter = b_col[:, 0:1] + m_prev
            m_t = jnp.maximum(inter, jnp.max(d, axis=-1, keepdims=True))
            w = jnp.exp(d - m_t)
            s_inter = jnp.exp(inter - m_t)

            qk = _dot(q, kt) * w
            state = c_ref[h].astype(BF16)
            ext = _dot(qk.astype(BF16), v_ext[:, :MLSTM_DV])
            carried = _dot(q, state)
            num = ext + s_inter * carried[:, :MLSTM_DV]
            den = jnp.sum(qk, axis=-1, keepdims=True) + s_inter * carried[:, MLSTM_DV:MLSTM_DV + 1]
            hid = num / jnp.maximum(jnp.abs(den), jnp.exp(-m_t))

            hn = _rms(hid, hn_ref[h:h + 1, :])
            og = o_ref[pl.ds(row0, L), h * MLSTM_DV:(h + 1) * MLSTM_DV]
            out_ref[pl.ds(row0, L), h * MLSTM_DV:(h + 1) * MLSTM_DV] = (jax.nn.sigmoid(og) * hn).astype(BF16)

            b_last = b_row[:, L - 1:L]
            g_row = b_last + a_row
            m_new = jnp.maximum(b_last + m_prev, jnp.max(g_row, axis=-1, keepdims=True))
            sc = jnp.exp(g_row - m_new)
            decay = jnp.exp(b_last + m_prev - m_new)
            kts = (kt.astype(F32) * sc).astype(BF16)
            c_ref[h] = decay * c_ref[h] + _dot(kts, v_ext)
            m_ref[h:h + 1, :] = jnp.broadcast_to(m_new, (1, 128))
        return carry

    lax.fori_loop(0, MLSTM_SEQ_TILE // L, chunk, 0)


def _mlstm_core(q, kt, v, o, gates, head_norm, batch, seq):
    t = q.shape[0]
    ns = seq // MLSTM_SEQ_TILE
    cps = MLSTM_SEQ_TILE // MLSTM_CHUNK

    def rows(width):
        return pl.BlockSpec((MLSTM_SEQ_TILE, width), lambda b, s: (b * ns + s, 0))

    def slabs(height):
        return pl.BlockSpec((cps, height, MLSTM_CHUNK), lambda b, s: (b * ns + s, 0, 0))

    return pl.pallas_call(
        _mlstm_core_body,
        grid=(batch, ns),
        in_specs=[rows(MLSTM_QK_W), slabs(MLSTM_QK_W), rows(D_MODEL), rows(D_MODEL), slabs(GATE_ROWS),
                  pl.BlockSpec(head_norm.shape, lambda b, s: (0, 0))],
        out_specs=rows(D_MODEL),
        out_shape=jax.ShapeDtypeStruct((t, D_MODEL), BF16),
        scratch_shapes=[pltpu.VMEM((MLSTM_HEADS, MLSTM_DK, MLSTM_DV + 128), F32),
                        pltpu.VMEM((8, 128), F32)],
        compiler_params=_params("parallel", "arbitrary"),
        name="mlstm_core",
    )(q, kt, v, o, gates, head_norm)


def _q_proj_body(x_ref, g_ref, w_ref, q_ref):
    xn = _rms(x_ref[...], g_ref[2:3, :]).astype(BF16)
    q_ref[...] = (_dot(xn, w_ref[...]) * (DIFF_HEAD_DIM ** -0.5)).astype(BF16)


def _q_proj(x, g, w_q):
    t = x.shape[0]
    return pl.pallas_call(
        _q_proj_body,
        grid=(t // TOKEN_TILE,),
        in_specs=[_row_tile(D_MODEL), _resident(g.shape), _resident(w_q.shape)],
        out_specs=_row_tile(D_MODEL),
        out_shape=jax.ShapeDtypeStruct((t, D_MODEL), BF16),
        compiler_params=_params("parallel"),
        name="q_proj",
    )(x, g, w_q)


def _attn_body(q_ref, k_ref, v_ref, lam_ref, subln_ref, o_ref, *, lam_init):
    tq, tk = ATTN_Q_TILE, ATTN_K_TILE
    qi = pl.program_id(2)
    q = q_ref[0]
    lane = lax.broadcasted_iota(jnp.int32, q.shape, 1)
    zero = jnp.zeros_like(q)
    qs = (jnp.where(lane < DIFF_HEAD_DIM, q, zero), jnp.where(lane >= DIFF_HEAD_DIM, q, zero))

    def step(j, carry, masked):
        k0 = pl.multiple_of(j * tk, tk)
        k = k_ref[0, pl.ds(k0, tk), :]
        v = v_ref[0, pl.ds(k0, tk), :]
        out = []
        for qm, (m, l, acc) in zip(qs, carry):
            s = _dot_nt(qm, k)
            if masked:
                r_i = lax.broadcasted_iota(jnp.int32, s.shape, 0)
                c_i = lax.broadcasted_iota(jnp.int32, s.shape, 1)
                s = jnp.where(c_i <= r_i, s, -jnp.inf)
            m_new = jnp.maximum(m, jnp.max(s, axis=-1, keepdims=True))
            alpha = jnp.exp(m - m_new)
            p = jnp.exp(s - m_new)
            l = alpha * l + jnp.sum(p, axis=-1, keepdims=True)
            acc = alpha * acc + _dot(p.astype(BF16), v)
            out.append((m_new, l, acc))
        return tuple(out)

    init = tuple((jnp.full((tq, 1), -jnp.inf, F32), jnp.zeros((tq, 1), F32),
                  jnp.zeros((tq, DIFF_PAIR_W), F32)) for _ in range(2))
    carry = lax.fori_loop(0, qi, lambda j, c: step(j, c, False), init)
    (_, l1, acc1), (_, l2, acc2) = step(qi, carry, True)

    lv = lam_ref[...]
    lam = (jnp.exp(jnp.sum(lv[0:1, :] * lv[1:2, :], axis=-1, keepdims=True))
           - jnp.exp(jnp.sum(lv[2:3, :] * lv[3:4, :], axis=-1, keepdims=True)) + lam_init)
    o = acc1 / l1 - lam * (acc2 / l2)
    o_ref[0] = (_rms(o, subln_ref[...]) * (1.0 - lam_init)).astype(BF16)


def _diff_attn(q, kv, lam_vecs, subln, lam_init, batch, seq):
    assert ATTN_Q_TILE == ATTN_K_TILE
    return pl.pallas_call(
        functools.partial(_attn_body, lam_init=lam_init),
        grid=(batch, DIFF_HEADS, seq // ATTN_Q_TILE),
        in_specs=[
            pl.BlockSpec((1, ATTN_Q_TILE, DIFF_PAIR_W), lambda b, h, i: (b, i, h)),
            pl.BlockSpec((1, seq, DIFF_PAIR_W), lambda b, h, i: (b, 0, h)),
            pl.BlockSpec((1, seq, DIFF_PAIR_W), lambda b, h, i: (b, 0, DIFF_HEADS + h)),
            pl.BlockSpec(lam_vecs.shape, lambda b, h, i: (0, 0)),
            pl.BlockSpec(subln.shape, lambda b, h, i: (0, 0)),
        ],
        out_specs=pl.BlockSpec((1, ATTN_Q_TILE, DIFF_PAIR_W), lambda b, h, i: (b, i, h)),
        out_shape=jax.ShapeDtypeStruct((batch, seq, D_MODEL), BF16),
        compiler_params=_params("parallel", "parallel", "arbitrary"),
        name="diff_attn",
    )(q, kv, kv, lam_vecs, subln)


def kernel(x, p, norm_g, w_ffn_in, w_ffn_out, w_ple_proj, w_ple_gate, mlstm_w_in, mlstm_b_gates,
           mlstm_head_norm, mlstm_w_out, kv_norm, w_kv, diff_w_q, diff_lambda, diff_subln, diff_w_out):
    batch, seq, _ = x.shape
    depth = norm_g.shape[0]
    n_a = depth // 2
    t = batch * seq
    assert t % TOKEN_TILE == 0 and seq % MLSTM_SEQ_TILE == 0 and seq % ATTN_Q_TILE == 0
    assert TOKEN_TILE % MLSTM_CHUNK == 0 and MLSTM_SEQ_TILE % TOKEN_TILE == 0

    bf = lambda w: w.astype(BF16)
    x = x.reshape(t, D_MODEL)
    p = p.reshape(depth, t, PLE_DIM)
    kv = None
    for layer in range(depth):
        g = norm_g[layer]
        x = _ffn(x, g, bf(w_ffn_in[layer, 0]), bf(w_ffn_out[layer, 0]), 0, 1)
        if layer < n_a:
            w_in = mlstm_w_in[layer]
            q_end, k_end = MLSTM_QK_W, 2 * MLSTM_QK_W
            v_end, o_end = k_end + D_MODEL, k_end + 2 * D_MODEL
            pad = GATE_ROWS - 2 * MLSTM_HEADS
            wgt = jnp.pad(w_in[:, o_end:].T, ((0, pad), (0, 0)))
            bg = jnp.pad(mlstm_b_gates[layer], (0, pad)).reshape(GATE_ROWS, 1)
            q, kt, v, o, gates = _mlstm_proj(
                x, g, bf(w_in[:, :q_end]), bf(w_in[:, q_end:k_end].T), bf(w_in[:, k_end:v_end]),
                bf(w_in[:, v_end:o_end]), bf(wgt), bg)
            h = _mlstm_core(q, kt, v, o, gates, mlstm_head_norm[layer], batch, seq)
            x = _mixer_out(x, h, g, bf(mlstm_w_out[layer]))
        else:
            j = layer - n_a
            lam_init = 0.8 - 0.6 * math.exp(-0.3 * layer)
            q = _q_proj(x, g, bf(diff_w_q[j]))
            h = _diff_attn(q.reshape(batch, seq, D_MODEL), kv.reshape(batch, seq, 2 * D_MODEL),
                           diff_lambda[j], diff_subln[j].reshape(1, DIFF_PAIR_W), lam_init, batch, seq)
            x = _mixer_out(x, h.reshape(t, D_MODEL), g, bf(diff_w_out[j]))
        x = _ffn(x, g, bf(w_ffn_in[layer, 1]), bf(w_ffn_out[layer, 1]), 4, 5)
        if layer == n_a - 1:
            x, kv = _ple(x, p[layer], g, bf(w_ple_gate[layer]), bf(w_ple_proj[layer]),
                         kv_norm.reshape(1, D_MODEL), bf(w_kv))
        else:
            x = _ple(x, p[layer], g, bf(w_ple_gate[layer]), bf(w_ple_proj[layer]))
    return x.reshape(batch, seq, D_MODEL)
```

```python
import functools
import math

import jax
import jax.numpy as jnp
from jax import lax
from jax.experimental import pallas as pl
from jax.experimental.pallas import tpu as pltpu

F32 = jnp.float32
BF16 = jnp.bfloat16

D_MODEL = 1024
D_FF = 2816
PLE_DIM = 256
EPS = 1e-6
LOG2E = math.log2(math.e)

MLSTM_HEADS = 4
MLSTM_DV = 256
MLSTM_DK = 128
MLSTM_CHUNK = 128
MLSTM_QK_W = MLSTM_HEADS * MLSTM_DK
GATE_ROWS = 16

DIFF_HEAD_DIM = 64
DIFF_HEADS = 8
DIFF_PAIR_W = 2 * DIFF_HEAD_DIM

VMEM_LIMIT_BYTES = 56 * 1024 * 1024

TOKEN_TILE = 512
FFN_CHUNK = 256
MLSTM_SEQ_TILE = 1024
ATTN_Q_TILE = 512
ATTN_K_TILE = 256
ATTN_SUM_ROWS = 16


def _rms(x, g):
    ms = jnp.mean(x * x, axis=-1, keepdims=True)
    return (x * lax.rsqrt(ms + EPS)) * g


def _dot(a, b):
    return jnp.dot(a, b, preferred_element_type=F32)


def _dot_nt(a, b):
    return lax.dot_general(a, b, (((1,), (1,)), ((), ())), preferred_element_type=F32)


def _params(*semantics):
    return pltpu.CompilerParams(dimension_semantics=semantics, vmem_limit_bytes=VMEM_LIMIT_BYTES)


def _resident(shape):
    nd = len(shape)
    return pl.BlockSpec(shape, lambda *_: (0,) * nd, pipeline_mode=pl.Buffered(1))


def _row_tile(width, tile=TOKEN_TILE):
    return pl.BlockSpec((tile, width), lambda i: (i, 0))


def _feature_major_tile(tiles_per_seq):
    return pl.BlockSpec((1, 1, D_MODEL, TOKEN_TILE), lambda i: (i // tiles_per_seq, i % tiles_per_seq, 0, 0))


def _ffn_body(x_ref, g_ref, win_ref, wout_ref, o_ref, *, pre, post):
    x = x_ref[...]
    xn = _rms(x, g_ref[pre:pre + 1, :]).astype(BF16)
    acc = jnp.zeros(x.shape, F32)
    for c in range(D_FF // FFN_CHUNK):
        lo = c * FFN_CHUNK
        gate = _dot(xn, win_ref[:, lo:lo + FFN_CHUNK])
        up = _dot(xn, win_ref[:, D_FF + lo:D_FF + lo + FFN_CHUNK])
        act = (gate * jax.nn.sigmoid(gate) * up).astype(BF16)
        acc = acc + _dot(act, wout_ref[lo:lo + FFN_CHUNK, :])
    o_ref[...] = x + 0.5 * _rms(acc, g_ref[post:post + 1, :])


def _ffn(x, g, w_in, w_out, pre, post):
    t = x.shape[0]
    return pl.pallas_call(
        functools.partial(_ffn_body, pre=pre, post=post),
        grid=(t // TOKEN_TILE,),
        in_specs=[_row_tile(D_MODEL), _resident(g.shape), _resident(w_in.shape), _resident(w_out.shape)],
        out_specs=_row_tile(D_MODEL),
        out_shape=jax.ShapeDtypeStruct(x.shape, F32),
        compiler_params=_params("parallel"),
        name="ffn",
    )(x, g, w_in, w_out)


def _ple_body(x_ref, p_ref, g_ref, wgate_ref, wple_ref, *rest, with_kv):
    if with_kv:
        kvn_ref, wk_ref, wvt_ref, o_ref, k_ref, vt_ref = rest
    else:
        (o_ref,) = rest
    x = x_ref[...]
    xn = _rms(x, g_ref[6:7, :]).astype(BF16)
    gate = jax.nn.sigmoid(_dot(xn, wgate_ref[...]))
    e = _dot(p_ref[...].astype(BF16), wple_ref[...])
    y = x + _rms(e * gate, g_ref[7:8, :])
    o_ref[...] = y
    if with_kv:
        yn = _rms(y, kvn_ref[...]).astype(BF16)
        k_ref[...] = _dot(yn, wk_ref[...]).astype(BF16)
        vt_ref[0, 0] = _dot_nt(wvt_ref[...], yn).astype(BF16)


def _ple(x, p, g, w_gate, w_ple, kv_norm=None, w_k=None, w_vt=None, batch=None):
    t = x.shape[0]
    with_kv = w_k is not None
    in_specs = [_row_tile(D_MODEL), _row_tile(PLE_DIM), _resident(g.shape),
                _resident(w_gate.shape), _resident(w_ple.shape)]
    args = [x, p, g, w_gate, w_ple]
    out_specs = _row_tile(D_MODEL)
    out_shape = jax.ShapeDtypeStruct(x.shape, F32)
    if with_kv:
        nsb = t // batch // TOKEN_TILE
        in_specs += [_resident(kv_norm.shape), _resident(w_k.shape), _resident(w_vt.shape)]
        args += [kv_norm, w_k, w_vt]
        out_specs = [out_specs, _row_tile(D_MODEL), _feature_major_tile(nsb)]
        out_shape = [out_shape, jax.ShapeDtypeStruct((t, D_MODEL), BF16),
                     jax.ShapeDtypeStruct((batch, nsb, D_MODEL, TOKEN_TILE), BF16)]
    return pl.pallas_call(
        functools.partial(_ple_body, with_kv=with_kv),
        grid=(t // TOKEN_TILE,),
        in_specs=in_specs,
        out_specs=out_specs,
        out_shape=out_shape,
        compiler_params=_params("parallel"),
        name="ple_kv" if with_kv else "ple",
    )(*args)


def _mixer_out_body(x_ref, h_ref, g_ref, w_ref, o_ref):
    x = x_ref[...]
    y = _dot(h_ref[...], w_ref[...])
    o_ref[...] = x + _rms(y, g_ref[3:4, :])


def _mixer_out(x, h, g, w_out):
    t = x.shape[0]
    return pl.pallas_call(
        _mixer_out_body,
        grid=(t // TOKEN_TILE,),
        in_specs=[_row_tile(D_MODEL), _row_tile(D_MODEL), _resident(g.shape), _resident(w_out.shape)],
        out_specs=_row_tile(D_MODEL),
        out_shape=jax.ShapeDtypeStruct(x.shape, F32),
        compiler_params=_params("parallel"),
        name="mixer_out",
    )(x, h, g, w_out)


def _mlstm_proj_body(x_ref, g_ref, wq_ref, wkt_ref, wv_ref, wo_ref, wgt_ref, bg_ref,
                     q_ref, kt_ref, v_ref, o_ref, gates_ref):
    xn = _rms(x_ref[...], g_ref[2:3, :]).astype(BF16)
    q_ref[...] = _dot(xn, wq_ref[...]).astype(BF16)
    v_ref[...] = _dot(xn, wv_ref[...]).astype(BF16)
    o_ref[...] = _dot(xn, wo_ref[...])
    kt = (_dot_nt(wkt_ref[...], xn) * (MLSTM_DK ** -0.5)).astype(BF16)
    z = _dot_nt(wgt_ref[...], xn) + bg_ref[...]
    row = lax.broadcasted_iota(jnp.int32, z.shape, 0)
    is_forget = (row >= MLSTM_HEADS) & (row < 2 * MLSTM_HEADS)
    gates = jnp.where(is_forget, jax.nn.log_sigmoid(z), z)
    for c in range(TOKEN_TILE // MLSTM_CHUNK):
        sl = slice(c * MLSTM_CHUNK, (c + 1) * MLSTM_CHUNK)
        kt_ref[c] = kt[:, sl]
        gates_ref[c] = gates[:, sl]


def _mlstm_proj(x, g, wq, wkt, wv, wo, wgt, bg):
    t = x.shape[0]
    cpt = TOKEN_TILE // MLSTM_CHUNK
    nchunks = t // MLSTM_CHUNK
    return pl.pallas_call(
        _mlstm_proj_body,
        grid=(t // TOKEN_TILE,),
        in_specs=[_row_tile(D_MODEL), _resident(g.shape), _resident(wq.shape), _resident(wkt.shape),
                  _resident(wv.shape), _resident(wo.shape), _resident(wgt.shape), _resident(bg.shape)],
        out_specs=[
            _row_tile(MLSTM_QK_W),
            pl.BlockSpec((cpt, MLSTM_QK_W, MLSTM_CHUNK), lambda i: (i, 0, 0)),
            _row_tile(D_MODEL),
            _row_tile(D_MODEL),
            pl.BlockSpec((cpt, GATE_ROWS, MLSTM_CHUNK), lambda i: (i, 0, 0)),
        ],
        out_shape=[
            jax.ShapeDtypeStruct((t, MLSTM_QK_W), BF16),
            jax.ShapeDtypeStruct((nchunks, MLSTM_QK_W, MLSTM_CHUNK), BF16),
            jax.ShapeDtypeStruct((t, D_MODEL), BF16),
            jax.ShapeDtypeStruct((t, D_MODEL), F32),
            jax.ShapeDtypeStruct((nchunks, GATE_ROWS, MLSTM_CHUNK), F32),
        ],
        compiler_params=_params("parallel"),
        name="mlstm_proj",
    )(x, g, wq, wkt, wv, wo, wgt, bg)


def _split3(x):
    hi = x.astype(BF16)
    r = x - hi.astype(F32)
    mid = r.astype(BF16)
    lo = (r - mid.astype(F32)).astype(BF16)
    return hi, mid, lo


def _mlstm_core_body(q_ref, kt_ref, v_ref, o_ref, gates_ref, hn_ref, out_ref, c_ref, m_ref):
    L = MLSTM_CHUNK

    @pl.when(pl.program_id(1) == 0)
    def _():
        c_ref[...] = jnp.zeros(c_ref.shape, F32)
        m_ref[...] = jnp.full(m_ref.shape, -jnp.inf, F32)

    r_i = lax.broadcasted_iota(jnp.int32, (L, L), 0)
    c_i = lax.broadcasted_iota(jnp.int32, (L, L), 1)
    causal = c_i <= r_i
    upper = (r_i <= c_i).astype(BF16)
    ones_blk = jnp.ones((L, 128), BF16)

    def chunk(c, carry):
        row0 = pl.multiple_of(c * L, L)
        g_all = gates_ref[c]
        b_all = sum(_dot(part, upper) for part in _split3(g_all))
        for h in range(MLSTM_HEADS):
            li = g_all[h:h + 1, :]
            b_row = b_all[MLSTM_HEADS + h:MLSTM_HEADS + h + 1, :]
            m_prev = m_ref[h:h + 1, 0:1]
            q = q_ref[pl.ds(row0, L), h * MLSTM_DK:(h + 1) * MLSTM_DK]
            kt = kt_ref[c, h * MLSTM_DK:(h + 1) * MLSTM_DK, :]
            v = v_ref[pl.ds(row0, L), h * MLSTM_DV:(h + 1) * MLSTM_DV]
            v_ext = jnp.concatenate([v, ones_blk], axis=1)

            b_col = jnp.broadcast_to(b_row, (L, L)).T
            a_row = li - b_row
            d = jnp.where(causal, b_col + a_row, -jnp.inf)
            inter = b_col[:, 0:1] + m_prev
            m_t = jnp.maximum(inter, jnp.max(d, axis=-1, keepdims=True))
            w = jnp.exp(d - m_t)
            s_inter = jnp.exp(inter - m_t)

            qk = _dot(q, kt) * w
            state = c_ref[h].astype(BF16)
            ext = _dot(qk.astype(BF16), v_ext[:, :MLSTM_DV])
            carried = _dot(q, state)
            num = ext + s_inter * carried[:, :MLSTM_DV]
            den = jnp.sum(qk, axis=-1, keepdims=True) + s_inter * carried[:, MLSTM_DV:MLSTM_DV + 1]
            hid = num / jnp.maximum(jnp.abs(den), jnp.exp(-m_t))

            hn = _rms(hid, hn_ref[h:h + 1, :])
            og = o_ref[pl.ds(row0, L), h * MLSTM_DV:(h + 1) * MLSTM_DV]
            out_ref[pl.ds(row0, L), h * MLSTM_DV:(h + 1) * MLSTM_DV] = (jax.nn.sigmoid(og) * hn).astype(BF16)

            b_last = b_row[:, L - 1:L]
            g_row = b_last + a_row
            m_new = jnp.maximum(b_last + m_prev, jnp.max(g_row, axis=-1, keepdims=True))
            sc = jnp.exp(g_row - m_new)
            decay = jnp.exp(b_last + m_prev - m_new)
            kts = (kt.astype(F32) * sc).astype(BF16)
            c_ref[h] = decay * c_ref[h] + _dot(kts, v_ext)
            m_ref[h:h + 1, :] = jnp.broadcast_to(m_new, (1, 128))
        return carry

    lax.fori_loop(0, MLSTM_SEQ_TILE // L, chunk, 0)


def _mlstm_core(q, kt, v, o, gates, head_norm, batch, seq):
    t = q.shape[0]
    ns = seq // MLSTM_SEQ_TILE
    cps = MLSTM_SEQ_TILE // MLSTM_CHUNK

    def rows(width):
        return pl.BlockSpec((MLSTM_SEQ_TILE, width), lambda b, s: (b * ns + s, 0))

    def slabs(height):
        return pl.BlockSpec((cps, height, MLSTM_CHUNK), lambda b, s: (b * ns + s, 0, 0))

    return pl.pallas_call(
        _mlstm_core_body,
        grid=(batch, ns),
        in_specs=[rows(MLSTM_QK_W), slabs(MLSTM_QK_W), rows(D_MODEL), rows(D_MODEL), slabs(GATE_ROWS),
                  pl.BlockSpec(head_norm.shape, lambda b, s: (0, 0))],
        out_specs=rows(D_MODEL),
        out_shape=jax.ShapeDtypeStruct((t, D_MODEL), BF16),
        scratch_shapes=[pltpu.VMEM((MLSTM_HEADS, MLSTM_DK, MLSTM_DV + 128), F32),
                        pltpu.VMEM((8, 128), F32)],
        compiler_params=_params("parallel", "arbitrary"),
        name="mlstm_core",
    )(q, kt, v, o, gates, head_norm)


def _q_proj_body(x_ref, g_ref, wt_ref, qt_ref):
    xn = _rms(x_ref[...], g_ref[2:3, :]).astype(BF16)
    qt_ref[0, 0] = (_dot_nt(wt_ref[...], xn) * (DIFF_HEAD_DIM ** -0.5 * LOG2E)).astype(BF16)


def _q_proj(x, g, w_qt, batch):
    t = x.shape[0]
    nsb = t // batch // TOKEN_TILE
    return pl.pallas_call(
        _q_proj_body,
        grid=(t // TOKEN_TILE,),
        in_specs=[_row_tile(D_MODEL), _resident(g.shape), _resident(w_qt.shape)],
        out_specs=_feature_major_tile(nsb),
        out_shape=jax.ShapeDtypeStruct((batch, nsb, D_MODEL, TOKEN_TILE), BF16),
        compiler_params=_params("parallel"),
        name="q_proj",
    )(x, g, w_qt)


def _attn_body(qt_ref, k_ref, vt_ref, lam_ref, subln_ref, o_ref,
               sa_ref, sb_ref, bma_ref, bmb_ref, m_ref, acc_ref, *, lam_init):
    tq, tk = ATTN_Q_TILE, ATTN_K_TILE
    qi = pl.program_id(2)
    qt = qt_ref[0, 0]
    row = lax.broadcasted_iota(jnp.int32, qt.shape, 0)
    zero = jnp.zeros_like(qt)
    qts = (jnp.where(row < DIFF_HEAD_DIM, qt, zero), jnp.where(row >= DIFF_HEAD_DIM, qt, zero))
    acc_ref[...] = jnp.zeros(acc_ref.shape, F32)
    m_ref[...] = jnp.full(m_ref.shape, -jnp.inf, F32)

    def score(pair, half, s_ref, bm_ref, mask_offset):
        k0 = pl.multiple_of(pair * tq + half * tk, tk)
        k = k_ref[0, pl.ds(k0, tk), :]
        for i, qm in enumerate(qts):
            s = _dot(k, qm)
            if mask_offset is not None:
                key = lax.broadcasted_iota(jnp.int32, s.shape, 0) + mask_offset
                qry = lax.broadcasted_iota(jnp.int32, s.shape, 1)
                s = jnp.where(key <= qry, s, -jnp.inf)
            s_ref[i] = s
            bm_ref[i] = jnp.max(s, axis=0, keepdims=True)

    ones_rows = jnp.ones((ATTN_SUM_ROWS, tk), BF16)

    def absorb(pair, half, s_ref, bm_ref):
        vt = vt_ref[0, pair, :, half * tk:(half + 1) * tk]
        vt_ext = jnp.concatenate([vt, ones_rows], axis=0)
        for i in range(2):
            m = m_ref[i]
            m_new = jnp.maximum(m, bm_ref[i])
            alpha = jnp.exp2(m - m_new)
            p = jnp.exp2(s_ref[i] - m_new)
            acc_ref[i] = alpha * acc_ref[i] + _dot(vt_ext, p.astype(BF16))
            m_ref[i] = m_new

    def pair_step(pair, next_is_diagonal):
        score(pair, 1, sb_ref, bmb_ref, None)
        absorb(pair, 0, sa_ref, bma_ref)
        score(pair + 1, 0, sa_ref, bma_ref, 0 if next_is_diagonal else None)
        absorb(pair, 1, sb_ref, bmb_ref)

    @pl.when(qi == 0)
    def _():
        score(0, 0, sa_ref, bma_ref, 0)

    @pl.when(qi > 0)
    def _():
        score(0, 0, sa_ref, bma_ref, None)

    def body(pair, carry):
        pair_step(pair, False)
        return carry

    lax.fori_loop(0, qi - 1, body, 0)

    @pl.when(qi > 0)
    def _():
        pair_step(qi - 1, True)

    score(qi, 1, sb_ref, bmb_ref, tk)
    absorb(qi, 0, sa_ref, bma_ref)
    absorb(qi, 1, sb_ref, bmb_ref)

    lv = lam_ref[...]
    lam = (jnp.exp(jnp.sum(lv[0:1, :] * lv[1:2, :], axis=-1, keepdims=True))
           - jnp.exp(jnp.sum(lv[2:3, :] * lv[3:4, :], axis=-1, keepdims=True)) + lam_init)
    dv = DIFF_PAIR_W
    ot = (acc_ref[0, :dv, :] / acc_ref[0, dv:dv + 1, :]
          - lam * (acc_ref[1, :dv, :] / acc_ref[1, dv:dv + 1, :]))
    o_ref[0] = (_rms(ot.T, subln_ref[...]) * (1.0 - lam_init)).astype(BF16)


def _diff_attn(qt, k, vt, lam_vecs, subln, lam_init):
    assert ATTN_Q_TILE == TOKEN_TILE and ATTN_Q_TILE == 2 * ATTN_K_TILE
    batch, seq, _ = k.shape
    nqb = seq // ATTN_Q_TILE
    stat = pltpu.VMEM((2, 1, ATTN_Q_TILE), F32)
    scores = pltpu.VMEM((2, ATTN_K_TILE, ATTN_Q_TILE), F32)
    return pl.pallas_call(
        functools.partial(_attn_body, lam_init=lam_init),
        grid=(batch, DIFF_HEADS, nqb),
        in_specs=[
            pl.BlockSpec((1, 1, DIFF_PAIR_W, ATTN_Q_TILE), lambda b, h, i: (b, i, h, 0)),
            pl.BlockSpec((1, seq, DIFF_PAIR_W), lambda b, h, i: (b, 0, h)),
            pl.BlockSpec((1, nqb, DIFF_PAIR_W, ATTN_Q_TILE), lambda b, h, i: (b, 0, h, 0)),
            pl.BlockSpec(lam_vecs.shape, lambda b, h, i: (0, 0)),
            pl.BlockSpec(subln.shape, lambda b, h, i: (0, 0)),
        ],
        out_specs=pl.BlockSpec((1, ATTN_Q_TILE, DIFF_PAIR_W), lambda b, h, i: (b, i, h)),
        out_shape=jax.ShapeDtypeStruct((batch, seq, D_MODEL), BF16),
        scratch_shapes=[scores, scores, stat, stat, stat,
                        pltpu.VMEM((2, DIFF_PAIR_W + ATTN_SUM_ROWS, ATTN_Q_TILE), F32)],
        compiler_params=_params("parallel", "parallel", "arbitrary"),
        name="diff_attn",
    )(qt, k, vt, lam_vecs, subln)


def kernel(x, p, norm_g, w_ffn_in, w_ffn_out, w_ple_proj, w_ple_gate, mlstm_w_in, mlstm_b_gates,
           mlstm_head_norm, mlstm_w_out, kv_norm, w_kv, diff_w_q, diff_lambda, diff_subln, diff_w_out):
    batch, seq, _ = x.shape
    depth = norm_g.shape[0]
    n_a = depth // 2
    t = batch * seq
    assert t % TOKEN_TILE == 0 and seq % MLSTM_SEQ_TILE == 0 and seq % ATTN_Q_TILE == 0
    assert TOKEN_TILE % MLSTM_CHUNK == 0 and MLSTM_SEQ_TILE % TOKEN_TILE == 0

    bf = lambda w: w.astype(BF16)
    x = x.reshape(t, D_MODEL)
    p = p.reshape(depth, t, PLE_DIM)
    k_sh = vt_sh = None
    for layer in range(depth):
        g = norm_g[layer]
        x = _ffn(x, g, bf(w_ffn_in[layer, 0]), bf(w_ffn_out[layer, 0]), 0, 1)
        if layer < n_a:
            w_in = mlstm_w_in[layer]
            q_end, k_end = MLSTM_QK_W, 2 * MLSTM_QK_W
            v_end, o_end = k_end + D_MODEL, k_end + 2 * D_MODEL
            pad = GATE_ROWS - 2 * MLSTM_HEADS
            wgt = jnp.pad(w_in[:, o_end:].T, ((0, pad), (0, 0)))
            bg = jnp.pad(mlstm_b_gates[layer], (0, pad)).reshape(GATE_ROWS, 1)
            q, kt, v, o, gates = _mlstm_proj(
                x, g, bf(w_in[:, :q_end]), bf(w_in[:, q_end:k_end].T), bf(w_in[:, k_end:v_end]),
                bf(w_in[:, v_end:o_end]), bf(wgt), bg)
            h = _mlstm_core(q, kt, v, o, gates, mlstm_head_norm[layer], batch, seq)
            x = _mixer_out(x, h, g, bf(mlstm_w_out[layer]))
        else:
            j = layer - n_a
            lam_init = 0.8 - 0.6 * math.exp(-0.3 * layer)
            qt = _q_proj(x, g, bf(diff_w_q[j].T), batch)
            h = _diff_attn(qt, k_sh.reshape(batch, seq, D_MODEL), vt_sh,
                           diff_lambda[j], diff_subln[j].reshape(1, DIFF_PAIR_W), lam_init)
            x = _mixer_out(x, h.reshape(t, D_MODEL), g, bf(diff_w_out[j]))
        x = _ffn(x, g, bf(w_ffn_in[layer, 1]), bf(w_ffn_out[layer, 1]), 4, 5)
        if layer == n_a - 1:
            x, k_sh, vt_sh = _ple(x, p[layer], g, bf(w_ple_gate[layer]), bf(w_ple_proj[layer]),
                                  kv_norm.reshape(1, D_MODEL), bf(w_kv[:, :D_MODEL]),
                                  bf(w_kv[:, D_MODEL:].T), batch)
        else:
            x = _ple(x, p[layer], g, bf(w_ple_gate[layer]), bf(w_ple_proj[layer]))
    return x.reshape(batch, seq, D_MODEL)
```

```python
import functools
import math

import jax
import jax.numpy as jnp
from jax import lax
from jax.experimental import pallas as pl
from jax.experimental.pallas import tpu as pltpu

F32 = jnp.float32
BF16 = jnp.bfloat16

D_MODEL = 1024
D_FF = 2816
PLE_DIM = 256
EPS = 1e-6
LOG2E = math.log2(math.e)

MLSTM_HEADS = 4
MLSTM_DV = 256
MLSTM_DK = 128
MLSTM_CHUNK = 128
MLSTM_QK_W = MLSTM_HEADS * MLSTM_DK
GATE_ROWS = 16

DIFF_HEAD_DIM = 64
DIFF_HEADS = 8
DIFF_PAIR_W = 2 * DIFF_HEAD_DIM

VMEM_LIMIT_BYTES = 56 * 1024 * 1024

TOKEN_TILE = 512
FFN_CHUNK = 256
MLSTM_SEQ_TILE = 1024
MLSTM_UNROLL = 2
ATTN_Q_TILE = 512
ATTN_K_TILE = 256
ATTN_SUM_ROWS = 16


def _rms(x, g):
    ms = jnp.mean(x * x, axis=-1, keepdims=True)
    return (x * lax.rsqrt(ms + EPS)) * g


def _dot(a, b):
    return jnp.dot(a, b, preferred_element_type=F32)


def _dot_nt(a, b):
    return lax.dot_general(a, b, (((1,), (1,)), ((), ())), preferred_element_type=F32)


def _params(*semantics):
    return pltpu.CompilerParams(dimension_semantics=semantics, vmem_limit_bytes=VMEM_LIMIT_BYTES)


def _resident(shape):
    nd = len(shape)
    return pl.BlockSpec(shape, lambda *_: (0,) * nd, pipeline_mode=pl.Buffered(1))


def _row_tile(width, tile=TOKEN_TILE):
    return pl.BlockSpec((tile, width), lambda i: (i, 0))


def _feature_major_tile(tiles_per_seq):
    return pl.BlockSpec((1, 1, D_MODEL, TOKEN_TILE), lambda i: (i // tiles_per_seq, i % tiles_per_seq, 0, 0))


def _ffn_math(x, g_ref, win_ref, wout_ref, pre, post):
    xn = _rms(x, g_ref[pre:pre + 1, :]).astype(BF16)
    acc = jnp.zeros(x.shape, F32)
    for c in range(D_FF // FFN_CHUNK):
        lo = c * FFN_CHUNK
        gate = _dot(xn, win_ref[:, lo:lo + FFN_CHUNK])
        up = _dot(xn, win_ref[:, D_FF + lo:D_FF + lo + FFN_CHUNK])
        act = (gate * jax.nn.sigmoid(gate) * up).astype(BF16)
        acc = acc + _dot(act, wout_ref[lo:lo + FFN_CHUNK, :])
    return x + 0.5 * _rms(acc, g_ref[post:post + 1, :])


def _mlstm_proj_math(x, g_ref, wq_ref, wkt_ref, wv_ref, wo_ref, wgt_ref, bg_ref,
                     q_ref, kt_ref, v_ref, o_ref, gates_ref):
    xn = _rms(x, g_ref[2:3, :]).astype(BF16)
    q_ref[...] = _dot(xn, wq_ref[...]).astype(BF16)
    v_ref[...] = _dot(xn, wv_ref[...]).astype(BF16)
    o_ref[...] = _dot(xn, wo_ref[...])
    kt = (_dot_nt(wkt_ref[...], xn) * (MLSTM_DK ** -0.5)).astype(BF16)
    z = _dot_nt(wgt_ref[...], xn) + bg_ref[...]
    row = lax.broadcasted_iota(jnp.int32, z.shape, 0)
    is_forget = (row >= MLSTM_HEADS) & (row < 2 * MLSTM_HEADS)
    gates = jnp.where(is_forget, jax.nn.log_sigmoid(z), z)
    for c in range(TOKEN_TILE // MLSTM_CHUNK):
        sl = slice(c * MLSTM_CHUNK, (c + 1) * MLSTM_CHUNK)
        kt_ref[c] = kt[:, sl]
        gates_ref[c] = gates[:, sl]


def _q_proj_math(x, g_ref, wt_ref, qt_ref):
    xn = _rms(x, g_ref[2:3, :]).astype(BF16)
    qt_ref[0, 0] = (_dot_nt(wt_ref[...], xn) * (DIFF_HEAD_DIM ** -0.5 * LOG2E)).astype(BF16)


def _pre_mixer_body(x_ref, g_ref, win_ref, wout_ref, *rest, mixer):
    *w_refs, y_ref = rest[:-5] if mixer == "mlstm" else rest[:-1]
    out_refs = rest[-5:] if mixer == "mlstm" else rest[-1:]
    y = _ffn_math(x_ref[...], g_ref, win_ref, wout_ref, 0, 1)
    y_ref[...] = y
    if mixer == "mlstm":
        _mlstm_proj_math(y, g_ref, *w_refs, *out_refs)
    else:
        _q_proj_math(y, g_ref, *w_refs, *out_refs)


def _pre_mixer(x, g, w_in, w_out, proj_weights, mixer, batch):
    t = x.shape[0]
    nsb = t // batch // TOKEN_TILE
    if mixer == "mlstm":
        cpt = TOKEN_TILE // MLSTM_CHUNK
        nchunks = t // MLSTM_CHUNK
        proj_specs = [
            _row_tile(MLSTM_QK_W),
            pl.BlockSpec((cpt, MLSTM_QK_W, MLSTM_CHUNK), lambda i: (i, 0, 0)),
            _row_tile(D_MODEL),
            _row_tile(D_MODEL),
            pl.BlockSpec((cpt, GATE_ROWS, MLSTM_CHUNK), lambda i: (i, 0, 0)),
        ]
        proj_shapes = [
            jax.ShapeDtypeStruct((t, MLSTM_QK_W), BF16),
            jax.ShapeDtypeStruct((nchunks, MLSTM_QK_W, MLSTM_CHUNK), BF16),
            jax.ShapeDtypeStruct((t, D_MODEL), BF16),
            jax.ShapeDtypeStruct((t, D_MODEL), F32),
            jax.ShapeDtypeStruct((nchunks, GATE_ROWS, MLSTM_CHUNK), F32),
        ]
    else:
        proj_specs = [_feature_major_tile(nsb)]
        proj_shapes = [jax.ShapeDtypeStruct((batch, nsb, D_MODEL, TOKEN_TILE), BF16)]
    return pl.pallas_call(
        functools.partial(_pre_mixer_body, mixer=mixer),
        grid=(t // TOKEN_TILE,),
        in_specs=[_row_tile(D_MODEL), _resident(g.shape), _resident(w_in.shape), _resident(w_out.shape)]
        + [_resident(w.shape) for w in proj_weights],
        out_specs=[_row_tile(D_MODEL)] + proj_specs,
        out_shape=[jax.ShapeDtypeStruct(x.shape, F32)] + proj_shapes,
        compiler_params=_params("parallel"),
        name="pre_" + mixer,
    )(x, g, w_in, w_out, *proj_weights)


def _post_mixer_body(x_ref, h_ref, p_ref, g_ref, wmix_ref, win_ref, wout_ref, wgate_ref, wple_ref,
                     *rest, with_kv):
    if with_kv:
        kvn_ref, wk_ref, wvt_ref, o_ref, k_ref, vt_ref = rest
    else:
        (o_ref,) = rest
    x = x_ref[...]
    x = x + _rms(_dot(h_ref[...], wmix_ref[...]), g_ref[3:4, :])
    x = _ffn_math(x, g_ref, win_ref, wout_ref, 4, 5)
    xn = _rms(x, g_ref[6:7, :]).astype(BF16)
    gate = jax.nn.sigmoid(_dot(xn, wgate_ref[...]))
    e = _dot(p_ref[...].astype(BF16), wple_ref[...])
    y = x + _rms(e * gate, g_ref[7:8, :])
    o_ref[...] = y
    if with_kv:
        yn = _rms(y, kvn_ref[...]).astype(BF16)
        k_ref[...] = _dot(yn, wk_ref[...]).astype(BF16)
        vt_ref[0, 0] = _dot_nt(wvt_ref[...], yn).astype(BF16)


def _post_mixer(x, h, p, g, w_mix, w_in, w_out, w_gate, w_ple, kv_norm=None, w_k=None, w_vt=None, batch=None):
    t = x.shape[0]
    with_kv = w_k is not None
    weights = [g, w_mix, w_in, w_out, w_gate, w_ple]
    out_specs = _row_tile(D_MODEL)
    out_shape = jax.ShapeDtypeStruct(x.shape, F32)
    if with_kv:
        nsb = t // batch // TOKEN_TILE
        weights += [kv_norm, w_k, w_vt]
        out_specs = [out_specs, _row_tile(D_MODEL), _feature_major_tile(nsb)]
        out_shape = [out_shape, jax.ShapeDtypeStruct((t, D_MODEL), BF16),
                     jax.ShapeDtypeStruct((batch, nsb, D_MODEL, TOKEN_TILE), BF16)]
    return pl.pallas_call(
        functools.partial(_post_mixer_body, with_kv=with_kv),
        grid=(t // TOKEN_TILE,),
        in_specs=[_row_tile(D_MODEL), _row_tile(D_MODEL), _row_tile(PLE_DIM)]
        + [_resident(w.shape) for w in weights],
        out_specs=out_specs,
        out_shape=out_shape,
        compiler_params=_params("parallel"),
        name="post_mixer_kv" if with_kv else "post_mixer",
    )(x, h, p, *weights)


def _split3(x):
    hi = x.astype(BF16)
    r = x - hi.astype(F32)
    mid = r.astype(BF16)
    lo = (r - mid.astype(F32)).astype(BF16)
    return hi, mid, lo


def _mlstm_core_body(q_ref, kt_ref, v_ref, o_ref, gates_ref, hn_ref, out_ref, c_ref, m_ref):
    L = MLSTM_CHUNK

    @pl.when(pl.program_id(1) == 0)
    def _():
        c_ref[...] = jnp.zeros(c_ref.shape, F32)
        m_ref[...] = jnp.full(m_ref.shape, -jnp.inf, F32)

    r_i = lax.broadcasted_iota(jnp.int32, (L, L), 0)
    c_i = lax.broadcasted_iota(jnp.int32, (L, L), 1)
    causal = c_i <= r_i
    upper = (r_i <= c_i).astype(BF16)
    ones_blk = jnp.ones((L, 128), BF16)

    def chunk(c):
        row0 = pl.multiple_of(c * L, L)
        g_all = gates_ref[c]
        b_all = sum(_dot(part, upper) for part in _split3(g_all))
        for h in range(MLSTM_HEADS):
            li = g_all[h:h + 1, :]
            b_row = b_all[MLSTM_HEADS + h:MLSTM_HEADS + h + 1, :]
            m_prev = m_ref[h:h + 1, 0:1]
            q = q_ref[pl.ds(row0, L), h * MLSTM_DK:(h + 1) * MLSTM_DK]
            kt = kt_ref[c, h * MLSTM_DK:(h + 1) * MLSTM_DK, :]
            v = v_ref[pl.ds(row0, L), h * MLSTM_DV:(h + 1) * MLSTM_DV]
            v_ext = jnp.concatenate([v, ones_blk], axis=1)

            b_col = jnp.broadcast_to(b_row, (L, L)).T
            a_row = li - b_row
            d = jnp.where(causal, b_col + a_row, -jnp.inf)
            inter = b_col[:, 0:1] + m_prev
            m_t = jnp.maximum(inter, jnp.max(d, axis=-1, keepdims=True))
            w = jnp.exp(d - m_t)
            s_inter = jnp.exp(inter - m_t)

            qk = _dot(q, kt) * w
            state = c_ref[h].astype(BF16)
            ext = _dot(qk.astype(BF16), v_ext[:, :MLSTM_DV])
            carried = _dot(q, state)
            num = ext + s_inter * carried[:, :MLSTM_DV]
            den = jnp.sum(qk, axis=-1, keepdims=True) + s_inter * carried[:, MLSTM_DV:MLSTM_DV + 1]
            hid = num / jnp.maximum(jnp.abs(den), jnp.exp(-m_t))

            hn = _rms(hid, hn_ref[h:h + 1, :])
            og = o_ref[pl.ds(row0, L), h * MLSTM_DV:(h + 1) * MLSTM_DV]
            out_ref[pl.ds(row0, L), h * MLSTM_DV:(h + 1) * MLSTM_DV] = (jax.nn.sigmoid(og) * hn).astype(BF16)

            b_last = b_row[:, L - 1:L]
            g_row = b_last + a_row
            m_new = jnp.maximum(b_last + m_prev, jnp.max(g_row, axis=-1, keepdims=True))
            sc = jnp.exp(g_row - m_new)
            decay = jnp.exp(b_last + m_prev - m_new)
            kts = (kt.astype(F32) * sc).astype(BF16)
            c_ref[h] = decay * c_ref[h] + _dot(kts, v_ext)
            m_ref[h:h + 1, :] = jnp.broadcast_to(m_new, (1, 128))

    def chunks(i, carry):
        for u in range(MLSTM_UNROLL):
            chunk(i * MLSTM_UNROLL + u)
        return carry

    lax.fori_loop(0, MLSTM_SEQ_TILE // (L * MLSTM_UNROLL), chunks, 0)


def _mlstm_core(q, kt, v, o, gates, head_norm, batch, seq):
    t = q.shape[0]
    ns = seq // MLSTM_SEQ_TILE
    cps = MLSTM_SEQ_TILE // MLSTM_CHUNK

    def rows(width):
        return pl.BlockSpec((MLSTM_SEQ_TILE, width), lambda b, s: (b * ns + s, 0))

    def slabs(height):
        return pl.BlockSpec((cps, height, MLSTM_CHUNK), lambda b, s: (b * ns + s, 0, 0))

    return pl.pallas_call(
        _mlstm_core_body,
        grid=(batch, ns),
        in_specs=[rows(MLSTM_QK_W), slabs(MLSTM_QK_W), rows(D_MODEL), rows(D_MODEL), slabs(GATE_ROWS),
                  pl.BlockSpec(head_norm.shape, lambda b, s: (0, 0))],
        out_specs=rows(D_MODEL),
        out_shape=jax.ShapeDtypeStruct((t, D_MODEL), BF16),
        scratch_shapes=[pltpu.VMEM((MLSTM_HEADS, MLSTM_DK, MLSTM_DV + 128), F32),
                        pltpu.VMEM((8, 128), F32)],
        compiler_params=_params("parallel", "arbitrary"),
        name="mlstm_core",
    )(q, kt, v, o, gates, head_norm)


def _attn_body(qt_ref, k_ref, vt_ref, lam_ref, subln_ref, o_ref,
               sa_ref, sb_ref, bma_ref, bmb_ref, m_ref, acc_ref, *, lam_init):
    tq, tk = ATTN_Q_TILE, ATTN_K_TILE
    qi = pl.program_id(2)
    qt = qt_ref[0, 0]
    row = lax.broadcasted_iota(jnp.int32, qt.shape, 0)
    zero = jnp.zeros_like(qt)
    qts = (jnp.where(row < DIFF_HEAD_DIM, qt, zero), jnp.where(row >= DIFF_HEAD_DIM, qt, zero))
    acc_ref[...] = jnp.zeros(acc_ref.shape, F32)
    m_ref[...] = jnp.full(m_ref.shape, -jnp.inf, F32)

    def score(pair, half, s_ref, bm_ref, mask_offset):
        k0 = pl.multiple_of(pair * tq + half * tk, tk)
        k = k_ref[0, pl.ds(k0, tk), :]
        for i, qm in enumerate(qts):
            s = _dot(k, qm)
            if mask_offset is not None:
                key = lax.broadcasted_iota(jnp.int32, s.shape, 0) + mask_offset
                qry = lax.broadcasted_iota(jnp.int32, s.shape, 1)
                s = jnp.where(key <= qry, s, -jnp.inf)
            s_ref[i] = s
            bm_ref[i] = jnp.max(s, axis=0, keepdims=True)

    ones_rows = jnp.ones((ATTN_SUM_ROWS, tk), BF16)

    def absorb(pair, half, s_ref, bm_ref):
        vt = vt_ref[0, pair, :, half * tk:(half + 1) * tk]
        vt_ext = jnp.concatenate([vt, ones_rows], axis=0)
        for i in range(2):
            m = m_ref[i]
            m_new = jnp.maximum(m, bm_ref[i])
            alpha = jnp.exp2(m - m_new)
            p = jnp.exp2(s_ref[i] - m_new)
            acc_ref[i] = alpha * acc_ref[i] + _dot(vt_ext, p.astype(BF16))
            m_ref[i] = m_new

    def pair_step(pair, next_is_diagonal):
        score(pair, 1, sb_ref, bmb_ref, None)
        absorb(pair, 0, sa_ref, bma_ref)
        score(pair + 1, 0, sa_ref, bma_ref, 0 if next_is_diagonal else None)
        absorb(pair, 1, sb_ref, bmb_ref)

    @pl.when(qi == 0)
    def _():
        score(0, 0, sa_ref, bma_ref, 0)
        score(0, 1, sb_ref, bmb_ref, tk)
        absorb(0, 0, sa_ref, bma_ref)
        absorb(0, 1, sb_ref, bmb_ref)

    @pl.when(qi > 0)
    def _():
        score(0, 0, sa_ref, bma_ref, None)
        n_plain = qi - 1

        def body(i, carry):
            pair_step(2 * i, False)
            pair_step(2 * i + 1, False)
            return carry

        lax.fori_loop(0, lax.shift_right_logical(n_plain, 1), body, 0)

        @pl.when((n_plain & 1) == 1)
        def _():
            pair_step(n_plain - 1, False)

        pair_step(qi - 1, True)
        score(qi, 1, sb_ref, bmb_ref, tk)
        absorb(qi, 0, sa_ref, bma_ref)
        absorb(qi, 1, sb_ref, bmb_ref)

    lv = lam_ref[...]
    lam = (jnp.exp(jnp.sum(lv[0:1, :] * lv[1:2, :], axis=-1, keepdims=True))
           - jnp.exp(jnp.sum(lv[2:3, :] * lv[3:4, :], axis=-1, keepdims=True)) + lam_init)
    dv = DIFF_PAIR_W
    ot = (acc_ref[0, :dv, :] / acc_ref[0, dv:dv + 1, :]
          - lam * (acc_ref[1, :dv, :] / acc_ref[1, dv:dv + 1, :]))
    o_ref[0] = (_rms(ot.T, subln_ref[...]) * (1.0 - lam_init)).astype(BF16)


def _diff_attn(qt, k, vt, lam_vecs, subln, lam_init):
    assert ATTN_Q_TILE == TOKEN_TILE and ATTN_Q_TILE == 2 * ATTN_K_TILE
    batch, seq, _ = k.shape
    nqb = seq // ATTN_Q_TILE
    stat = pltpu.VMEM((2, 1, ATTN_Q_TILE), F32)
    scores = pltpu.VMEM((2, ATTN_K_TILE, ATTN_Q_TILE), F32)
    return pl.pallas_call(
        functools.partial(_attn_body, lam_init=lam_init),
        grid=(batch, DIFF_HEADS, nqb),
        in_specs=[
            pl.BlockSpec((1, 1, DIFF_PAIR_W, ATTN_Q_TILE), lambda b, h, i: (b, i, h, 0)),
            pl.BlockSpec((1, seq, DIFF_PAIR_W), lambda b, h, i: (b, 0, h)),
            pl.BlockSpec((1, nqb, DIFF_PAIR_W, ATTN_Q_TILE), lambda b, h, i: (b, 0, h, 0)),
            pl.BlockSpec(lam_vecs.shape, lambda b, h, i: (0, 0)),
            pl.BlockSpec(subln.shape, lambda b, h, i: (0, 0)),
        ],
        out_specs=pl.BlockSpec((1, ATTN_Q_TILE, DIFF_PAIR_W), lambda b, h, i: (b, i, h)),
        out_shape=jax.ShapeDtypeStruct((batch, seq, D_MODEL), BF16),
        scratch_shapes=[scores, scores, stat, stat, stat,
                        pltpu.VMEM((2, DIFF_PAIR_W + ATTN_SUM_ROWS, ATTN_Q_TILE), F32)],
        compiler_params=_params("parallel", "parallel", "arbitrary"),
        name="diff_attn",
    )(qt, k, vt, lam_vecs, subln)


def kernel(x, p, norm_g, w_ffn_in, w_ffn_out, w_ple_proj, w_ple_gate, mlstm_w_in, mlstm_b_gates,
           mlstm_head_norm, mlstm_w_out, kv_norm, w_kv, diff_w_q, diff_lambda, diff_subln, diff_w_out):
    batch, seq, _ = x.shape
    depth = norm_g.shape[0]
    n_a = depth // 2
    t = batch * seq
    assert t % TOKEN_TILE == 0 and seq % MLSTM_SEQ_TILE == 0 and seq % ATTN_Q_TILE == 0
    assert TOKEN_TILE % MLSTM_CHUNK == 0 and MLSTM_SEQ_TILE % TOKEN_TILE == 0

    bf = lambda w: w.astype(BF16)
    x = x.reshape(t, D_MODEL)
    p = p.reshape(depth, t, PLE_DIM)
    k_sh = vt_sh = None
    for layer in range(depth):
        g = norm_g[layer]
        ffn1 = (bf(w_ffn_in[layer, 0]), bf(w_ffn_out[layer, 0]))
        ffn2 = (bf(w_ffn_in[layer, 1]), bf(w_ffn_out[layer, 1]))
        ple = (bf(w_ple_gate[layer]), bf(w_ple_proj[layer]))
        if layer < n_a:
            w_in = mlstm_w_in[layer]
            q_end, k_end = MLSTM_QK_W, 2 * MLSTM_QK_W
            v_end, o_end = k_end + D_MODEL, k_end + 2 * D_MODEL
            pad = GATE_ROWS - 2 * MLSTM_HEADS
            wgt = jnp.pad(w_in[:, o_end:].T, ((0, pad), (0, 0)))
            bg = jnp.pad(mlstm_b_gates[layer], (0, pad)).reshape(GATE_ROWS, 1)
            proj = (bf(w_in[:, :q_end]), bf(w_in[:, q_end:k_end].T), bf(w_in[:, k_end:v_end]),
                    bf(w_in[:, v_end:o_end]), bf(wgt), bg)
            x, q, kt, v, o, gates = _pre_mixer(x, g, *ffn1, proj, "mlstm", batch)
            h = _mlstm_core(q, kt, v, o, gates, mlstm_head_norm[layer], batch, seq)
            w_mix = bf(mlstm_w_out[layer])
        else:
            j = layer - n_a
            lam_init = 0.8 - 0.6 * math.exp(-0.3 * layer)
            x, qt = _pre_mixer(x, g, *ffn1, (bf(diff_w_q[j].T),), "attn", batch)
            h = _diff_attn(qt, k_sh.reshape(batch, seq, D_MODEL), vt_sh,
                           diff_lambda[j], diff_subln[j].reshape(1, DIFF_PAIR_W), lam_init)
            h = h.reshape(t, D_MODEL)
            w_mix = bf(diff_w_out[j])
        if layer == n_a - 1:
            x, k_sh, vt_sh = _post_mixer(x, h, p[layer], g, w_mix, *ffn2, *ple,
                                         kv_norm.reshape(1, D_MODEL), bf(w_kv[:, :D_MODEL]),
                                         bf(w_kv[:, D_MODEL:].T), batch)
        else:
            x = _post_mixer(x, h, p[layer], g, w_mix, *ffn2, *ple)
    return x.reshape(batch, seq, D_MODEL)
```

```python
import functools
import math

import jax
import jax.numpy as jnp
from jax import lax
from jax.experimental import pallas as pl
from jax.experimental.pallas import tpu as pltpu

F32 = jnp.float32
BF16 = jnp.bfloat16

D_MODEL = 1024
D_FF = 2816
PLE_DIM = 256
EPS = 1e-6
LOG2E = math.log2(math.e)

MLSTM_HEADS = 4
MLSTM_DV = 256
MLSTM_DK = 128
MLSTM_CHUNK = 128
MLSTM_QK_W = MLSTM_HEADS * MLSTM_DK
GATE_ROWS = 16

DIFF_HEAD_DIM = 64
DIFF_HEADS = 8
DIFF_PAIR_W = 2 * DIFF_HEAD_DIM

VMEM_LIMIT_BYTES = 56 * 1024 * 1024

TOKEN_TILE = 512
FFN_CHUNK = 256
MLSTM_SEQ_TILE = 1024
MLSTM_UNROLL = 2
ATTN_Q_TILE = 512
ATTN_K_TILE = 256
ATTN_SUM_ROWS = 16
ATTN_HEADS_PER_STEP = 2


def _rms(x, g):
    ms = jnp.mean(x * x, axis=-1, keepdims=True)
    return (x * lax.rsqrt(ms + EPS)) * g


def _dot(a, b):
    return jnp.dot(a, b, preferred_element_type=F32)


def _dot_nt(a, b):
    return lax.dot_general(a, b, (((1,), (1,)), ((), ())), preferred_element_type=F32)


def _params(*semantics):
    return pltpu.CompilerParams(dimension_semantics=semantics, vmem_limit_bytes=VMEM_LIMIT_BYTES)


def _resident(shape):
    nd = len(shape)
    return pl.BlockSpec(shape, lambda *_: (0,) * nd, pipeline_mode=pl.Buffered(1))


def _row_tile(width, tile=TOKEN_TILE):
    return pl.BlockSpec((tile, width), lambda i: (i, 0))


def _feature_major_tile(tiles_per_seq):
    return pl.BlockSpec((1, 1, D_MODEL, TOKEN_TILE), lambda i: (i // tiles_per_seq, i % tiles_per_seq, 0, 0))


def _ffn_math(x, g_ref, win_ref, wout_ref, pre, post):
    xn = _rms(x, g_ref[pre:pre + 1, :]).astype(BF16)
    acc = jnp.zeros(x.shape, F32)
    for c in range(D_FF // FFN_CHUNK):
        lo = c * FFN_CHUNK
        gate = _dot(xn, win_ref[:, lo:lo + FFN_CHUNK])
        up = _dot(xn, win_ref[:, D_FF + lo:D_FF + lo + FFN_CHUNK])
        act = (gate * jax.nn.sigmoid(gate) * up).astype(BF16)
        acc = acc + _dot(act, wout_ref[lo:lo + FFN_CHUNK, :])
    return x + 0.5 * _rms(acc, g_ref[post:post + 1, :])


def _mlstm_proj_math(x, g_ref, wq_ref, wkt_ref, wv_ref, wo_ref, wgt_ref, bg_ref,
                     q_ref, kt_ref, v_ref, o_ref, gates_ref):
    xn = _rms(x, g_ref[2:3, :]).astype(BF16)
    q_ref[...] = _dot(xn, wq_ref[...]).astype(BF16)
    v_ref[...] = _dot(xn, wv_ref[...]).astype(BF16)
    o_ref[...] = _dot(xn, wo_ref[...])
    kt = (_dot_nt(wkt_ref[...], xn) * (MLSTM_DK ** -0.5)).astype(BF16)
    z = _dot_nt(wgt_ref[...], xn) + bg_ref[...]
    row = lax.broadcasted_iota(jnp.int32, z.shape, 0)
    is_forget = (row >= MLSTM_HEADS) & (row < 2 * MLSTM_HEADS)
    gates = jnp.where(is_forget, jax.nn.log_sigmoid(z), z)
    for c in range(TOKEN_TILE // MLSTM_CHUNK):
        sl = slice(c * MLSTM_CHUNK, (c + 1) * MLSTM_CHUNK)
        kt_ref[c] = kt[:, sl]
        gates_ref[c] = gates[:, sl]


def _q_proj_math(x, g_ref, wt_ref, qt_ref):
    xn = _rms(x, g_ref[2:3, :]).astype(BF16)
    qt_ref[0, 0] = (_dot_nt(wt_ref[...], xn) * (DIFF_HEAD_DIM ** -0.5 * LOG2E)).astype(BF16)


def _pre_mixer_body(x_ref, g_ref, win_ref, wout_ref, *rest, mixer):
    *w_refs, y_ref = rest[:-5] if mixer == "mlstm" else rest[:-1]
    out_refs = rest[-5:] if mixer == "mlstm" else rest[-1:]
    y = _ffn_math(x_ref[...], g_ref, win_ref, wout_ref, 0, 1)
    y_ref[...] = y
    if mixer == "mlstm":
        _mlstm_proj_math(y, g_ref, *w_refs, *out_refs)
    else:
        _q_proj_math(y, g_ref, *w_refs, *out_refs)


def _pre_mixer(x, g, w_in, w_out, proj_weights, mixer, batch):
    t = x.shape[0]
    nsb = t // batch // TOKEN_TILE
    if mixer == "mlstm":
        cpt = TOKEN_TILE // MLSTM_CHUNK
        nchunks = t // MLSTM_CHUNK
        proj_specs = [
            _row_tile(MLSTM_QK_W),
            pl.BlockSpec((cpt, MLSTM_QK_W, MLSTM_CHUNK), lambda i: (i, 0, 0)),
            _row_tile(D_MODEL),
            _row_tile(D_MODEL),
            pl.BlockSpec((cpt, GATE_ROWS, MLSTM_CHUNK), lambda i: (i, 0, 0)),
        ]
        proj_shapes = [
            jax.ShapeDtypeStruct((t, MLSTM_QK_W), BF16),
            jax.ShapeDtypeStruct((nchunks, MLSTM_QK_W, MLSTM_CHUNK), BF16),
            jax.ShapeDtypeStruct((t, D_MODEL), BF16),
            jax.ShapeDtypeStruct((t, D_MODEL), F32),
            jax.ShapeDtypeStruct((nchunks, GATE_ROWS, MLSTM_CHUNK), F32),
        ]
    else:
        proj_specs = [_feature_major_tile(nsb)]
        proj_shapes = [jax.ShapeDtypeStruct((batch, nsb, D_MODEL, TOKEN_TILE), BF16)]
    return pl.pallas_call(
        functools.partial(_pre_mixer_body, mixer=mixer),
        grid=(t // TOKEN_TILE,),
        in_specs=[_row_tile(D_MODEL), _resident(g.shape), _resident(w_in.shape), _resident(w_out.shape)]
        + [_resident(w.shape) for w in proj_weights],
        out_specs=[_row_tile(D_MODEL)] + proj_specs,
        out_shape=[jax.ShapeDtypeStruct(x.shape, F32)] + proj_shapes,
        compiler_params=_params("parallel"),
        name="pre_" + mixer,
    )(x, g, w_in, w_out, *proj_weights)


def _post_mixer_body(x_ref, h_ref, p_ref, g_ref, wmix_ref, win_ref, wout_ref, wgate_ref, wple_ref,
                     *rest, with_kv, h_feature_major):
    if with_kv:
        kvn_ref, wk_ref, wvt_ref, o_ref, k_ref, vt_ref = rest
    else:
        (o_ref,) = rest
    x = x_ref[...]
    if h_feature_major:
        mixed = lax.dot_general(h_ref[0, 0], wmix_ref[...], (((0,), (0,)), ((), ())),
                                preferred_element_type=F32)
    else:
        mixed = _dot(h_ref[...], wmix_ref[...])
    x = x + _rms(mixed, g_ref[3:4, :])
    x = _ffn_math(x, g_ref, win_ref, wout_ref, 4, 5)
    xn = _rms(x, g_ref[6:7, :]).astype(BF16)
    gate = jax.nn.sigmoid(_dot(xn, wgate_ref[...]))
    e = _dot(p_ref[...].astype(BF16), wple_ref[...])
    y = x + _rms(e * gate, g_ref[7:8, :])
    o_ref[...] = y
    if with_kv:
        yn = _rms(y, kvn_ref[...]).astype(BF16)
        k_ref[...] = _dot(yn, wk_ref[...]).astype(BF16)
        vt_ref[0, 0] = _dot_nt(wvt_ref[...], yn).astype(BF16)


def _post_mixer(x, h, p, g, w_mix, w_in, w_out, w_gate, w_ple, kv_norm=None, w_k=None, w_vt=None, batch=None):
    t = x.shape[0]
    with_kv = w_k is not None
    h_feature_major = h.ndim == 4
    h_spec = _feature_major_tile(h.shape[1]) if h_feature_major else _row_tile(D_MODEL)
    weights = [g, w_mix, w_in, w_out, w_gate, w_ple]
    out_specs = _row_tile(D_MODEL)
    out_shape = jax.ShapeDtypeStruct(x.shape, F32)
    if with_kv:
        nsb = t // batch // TOKEN_TILE
        weights += [kv_norm, w_k, w_vt]
        out_specs = [out_specs, _row_tile(D_MODEL), _feature_major_tile(nsb)]
        out_shape = [out_shape, jax.ShapeDtypeStruct((t, D_MODEL), BF16),
                     jax.ShapeDtypeStruct((batch, nsb, D_MODEL, TOKEN_TILE), BF16)]
    return pl.pallas_call(
        functools.partial(_post_mixer_body, with_kv=with_kv, h_feature_major=h_feature_major),
        grid=(t // TOKEN_TILE,),
        in_specs=[_row_tile(D_MODEL), h_spec, _row_tile(PLE_DIM)]
        + [_resident(w.shape) for w in weights],
        out_specs=out_specs,
        out_shape=out_shape,
        compiler_params=_params("parallel"),
        name="post_mixer_kv" if with_kv else "post_mixer",
    )(x, h, p, *weights)


def _split3(x):
    hi = x.astype(BF16)
    r = x - hi.astype(F32)
    mid = r.astype(BF16)
    lo = (r - mid.astype(F32)).astype(BF16)
    return hi, mid, lo


def _mlstm_core_body(q_ref, kt_ref, v_ref, o_ref, gates_ref, hn_ref, out_ref, c_ref, m_ref):
    L = MLSTM_CHUNK

    @pl.when(pl.program_id(1) == 0)
    def _():
        c_ref[...] = jnp.zeros(c_ref.shape, F32)
        m_ref[...] = jnp.full(m_ref.shape, -jnp.inf, F32)

    r_i = lax.broadcasted_iota(jnp.int32, (L, L), 0)
    c_i = lax.broadcasted_iota(jnp.int32, (L, L), 1)
    causal = c_i <= r_i
    upper = (r_i <= c_i).astype(BF16)
    ones_blk = jnp.ones((L, 128), BF16)

    def chunk(c):
        row0 = pl.multiple_of(c * L, L)
        g_all = gates_ref[c]
        b_all = sum(_dot(part, upper) for part in _split3(g_all))
        for h in range(MLSTM_HEADS):
            li = g_all[h:h + 1, :]
            b_row = b_all[MLSTM_HEADS + h:MLSTM_HEADS + h + 1, :]
            m_prev = m_ref[h:h + 1, 0:1]
            q = q_ref[pl.ds(row0, L), h * MLSTM_DK:(h + 1) * MLSTM_DK]
            kt = kt_ref[c, h * MLSTM_DK:(h + 1) * MLSTM_DK, :]
            v = v_ref[pl.ds(row0, L), h * MLSTM_DV:(h + 1) * MLSTM_DV]
            v_ext = jnp.concatenate([v, ones_blk], axis=1)

            b_col = jnp.broadcast_to(b_row, (L, L)).T
            a_row = li - b_row
            d = jnp.where(causal, b_col + a_row, -jnp.inf)
            inter = b_col[:, 0:1] + m_prev
            m_t = jnp.maximum(inter, jnp.max(d, axis=-1, keepdims=True))
            w = jnp.exp(d - m_t)
            s_inter = jnp.exp(inter - m_t)

            qk = _dot(q, kt) * w
            state = c_ref[h].astype(BF16)
            ext = _dot(qk.astype(BF16), v_ext[:, :MLSTM_DV])
            carried = _dot(q, state)
            num = ext + s_inter * carried[:, :MLSTM_DV]
            den = jnp.sum(qk, axis=-1, keepdims=True) + s_inter * carried[:, MLSTM_DV:MLSTM_DV + 1]
            hid = num / jnp.maximum(jnp.abs(den), jnp.exp(-m_t))

            hn = _rms(hid, hn_ref[h:h + 1, :])
            og = o_ref[pl.ds(row0, L), h * MLSTM_DV:(h + 1) * MLSTM_DV]
            out_ref[pl.ds(row0, L), h * MLSTM_DV:(h + 1) * MLSTM_DV] = (jax.nn.sigmoid(og) * hn).astype(BF16)

            b_last = b_row[:, L - 1:L]
            g_row = b_last + a_row
            m_new = jnp.maximum(b_last + m_prev, jnp.max(g_row, axis=-1, keepdims=True))
            sc = jnp.exp(g_row - m_new)
            decay = jnp.exp(b_last + m_prev - m_new)
            kts = (kt.astype(F32) * sc).astype(BF16)
            c_ref[h] = decay * c_ref[h] + _dot(kts, v_ext)
            m_ref[h:h + 1, :] = jnp.broadcast_to(m_new, (1, 128))

    def chunks(i, carry):
        for u in range(MLSTM_UNROLL):
            chunk(i * MLSTM_UNROLL + u)
        return carry

    lax.fori_loop(0, MLSTM_SEQ_TILE // (L * MLSTM_UNROLL), chunks, 0)


def _mlstm_core(q, kt, v, o, gates, head_norm, batch, seq):
    t = q.shape[0]
    ns = seq // MLSTM_SEQ_TILE
    cps = MLSTM_SEQ_TILE // MLSTM_CHUNK

    def rows(width):
        return pl.BlockSpec((MLSTM_SEQ_TILE, width), lambda b, s: (b * ns + s, 0))

    def slabs(height):
        return pl.BlockSpec((cps, height, MLSTM_CHUNK), lambda b, s: (b * ns + s, 0, 0))

    return pl.pallas_call(
        _mlstm_core_body,
        grid=(batch, ns),
        in_specs=[rows(MLSTM_QK_W), slabs(MLSTM_QK_W), rows(D_MODEL), rows(D_MODEL), slabs(GATE_ROWS),
                  pl.BlockSpec(head_norm.shape, lambda b, s: (0, 0))],
        out_specs=rows(D_MODEL),
        out_shape=jax.ShapeDtypeStruct((t, D_MODEL), BF16),
        scratch_shapes=[pltpu.VMEM((MLSTM_HEADS, MLSTM_DK, MLSTM_DV + 128), F32),
                        pltpu.VMEM((8, 128), F32)],
        compiler_params=_params("parallel", "arbitrary"),
        name="mlstm_core",
    )(q, kt, v, o, gates, head_norm)


def _attn_body(qt_ref, k_ref, vt_ref, lam_ref, subln_ref, o_ref,
               sa_ref, sb_ref, bma_ref, bmb_ref, m_ref, acc_ref, *, lam_init):
    tq, tk = ATTN_Q_TILE, ATTN_K_TILE
    dv = DIFF_PAIR_W
    n_chains = 2 * ATTN_HEADS_PER_STEP
    qi = pl.program_id(2)
    qt = qt_ref[0, 0]
    row = lax.broadcasted_iota(jnp.int32, qt.shape, 0)
    zero = jnp.zeros_like(qt)
    qts = [jnp.where((row >= c * DIFF_HEAD_DIM) & (row < (c + 1) * DIFF_HEAD_DIM), qt, zero)
           for c in range(n_chains)]
    acc_ref[...] = jnp.zeros(acc_ref.shape, F32)
    m_ref[...] = jnp.full(m_ref.shape, -jnp.inf, F32)

    def score(pair, half, s_ref, bm_ref, mask_offset):
        k0 = pl.multiple_of(pair * tq + half * tk, tk)
        k = k_ref[0, pl.ds(k0, tk), :]
        for c, qm in enumerate(qts):
            s = _dot(k, qm)
            if mask_offset is not None:
                key = lax.broadcasted_iota(jnp.int32, s.shape, 0) + mask_offset
                qry = lax.broadcasted_iota(jnp.int32, s.shape, 1)
                s = jnp.where(key <= qry, s, -jnp.inf)
            s_ref[c] = s
            bm_ref[c] = jnp.max(s, axis=0, keepdims=True)

    ones_rows = jnp.ones((ATTN_SUM_ROWS, tk), BF16)

    def absorb(pair, half, s_ref, bm_ref):
        for c in range(n_chains):
            head = c // 2
            vt = vt_ref[0, pair, head * dv:(head + 1) * dv, half * tk:(half + 1) * tk]
            vt_ext = jnp.concatenate([vt, ones_rows], axis=0)
            m = m_ref[c]
            m_new = jnp.maximum(m, bm_ref[c])
            alpha = jnp.exp2(m - m_new)
            p = jnp.exp2(s_ref[c] - m_new)
            acc_ref[c] = alpha * acc_ref[c] + _dot(vt_ext, p.astype(BF16))
            m_ref[c] = m_new

    def pair_step(pair, next_is_diagonal):
        score(pair, 1, sb_ref, bmb_ref, None)
        absorb(pair, 0, sa_ref, bma_ref)
        score(pair + 1, 0, sa_ref, bma_ref, 0 if next_is_diagonal else None)
        absorb(pair, 1, sb_ref, bmb_ref)

    @pl.when(qi == 0)
    def _():
        score(0, 0, sa_ref, bma_ref, 0)
        score(0, 1, sb_ref, bmb_ref, tk)
        absorb(0, 0, sa_ref, bma_ref)
        absorb(0, 1, sb_ref, bmb_ref)

    @pl.when(qi > 0)
    def _():
        score(0, 0, sa_ref, bma_ref, None)
        n_plain = qi - 1

        def body(i, carry):
            pair_step(2 * i, False)
            pair_step(2 * i + 1, False)
            return carry

        lax.fori_loop(0, lax.shift_right_logical(n_plain, 1), body, 0)

        @pl.when((n_plain & 1) == 1)
        def _():
            pair_step(n_plain - 1, False)

        pair_step(qi - 1, True)
        score(qi, 1, sb_ref, bmb_ref, tk)
        absorb(qi, 0, sa_ref, bma_ref)
        absorb(qi, 1, sb_ref, bmb_ref)

    lv = lam_ref[...]
    lam = (jnp.exp(jnp.sum(lv[0:1, :] * lv[1:2, :], axis=-1, keepdims=True))
           - jnp.exp(jnp.sum(lv[2:3, :] * lv[3:4, :], axis=-1, keepdims=True)) + lam_init)
    for head in range(ATTN_HEADS_PER_STEP):
        c1, c2 = 2 * head, 2 * head + 1
        ot = (acc_ref[c1, :dv, :] / acc_ref[c1, dv:dv + 1, :]
              - lam * (acc_ref[c2, :dv, :] / acc_ref[c2, dv:dv + 1, :]))
        ms = jnp.mean(ot * ot, axis=0, keepdims=True)
        o_ref[0, 0, head * dv:(head + 1) * dv, :] = (
            ((ot * lax.rsqrt(ms + EPS)) * subln_ref[...]) * (1.0 - lam_init)).astype(BF16)


def _diff_attn(qt, k, vt, lam_vecs, subln, lam_init):
    assert ATTN_Q_TILE == TOKEN_TILE and ATTN_Q_TILE == 2 * ATTN_K_TILE
    batch, seq, _ = k.shape
    nqb = seq // ATTN_Q_TILE
    n_chains = 2 * ATTN_HEADS_PER_STEP
    width = ATTN_HEADS_PER_STEP * DIFF_PAIR_W
    stat = pltpu.VMEM((n_chains, 1, ATTN_Q_TILE), F32)
    scores = pltpu.VMEM((n_chains, ATTN_K_TILE, ATTN_Q_TILE), F32)
    return pl.pallas_call(
        functools.partial(_attn_body, lam_init=lam_init),
        grid=(batch, DIFF_HEADS // ATTN_HEADS_PER_STEP, nqb),
        in_specs=[
            pl.BlockSpec((1, 1, width, ATTN_Q_TILE), lambda b, h, i: (b, i, h, 0)),
            pl.BlockSpec((1, seq, width), lambda b, h, i: (b, 0, h)),
            pl.BlockSpec((1, nqb, width, ATTN_Q_TILE), lambda b, h, i: (b, 0, h, 0)),
            pl.BlockSpec(lam_vecs.shape, lambda b, h, i: (0, 0)),
            pl.BlockSpec(subln.shape, lambda b, h, i: (0, 0)),
        ],
        out_specs=pl.BlockSpec((1, 1, width, ATTN_Q_TILE), lambda b, h, i: (b, i, h, 0)),
        out_shape=jax.ShapeDtypeStruct((batch, nqb, D_MODEL, ATTN_Q_TILE), BF16),
        scratch_shapes=[scores, scores, stat, stat, stat,
                        pltpu.VMEM((n_chains, DIFF_PAIR_W + ATTN_SUM_ROWS, ATTN_Q_TILE), F32)],
        compiler_params=_params("parallel", "parallel", "arbitrary"),
        name="diff_attn",
    )(qt, k, vt, lam_vecs, subln)


def kernel(x, p, norm_g, w_ffn_in, w_ffn_out, w_ple_proj, w_ple_gate, mlstm_w_in, mlstm_b_gates,
           mlstm_head_norm, mlstm_w_out, kv_norm, w_kv, diff_w_q, diff_lambda, diff_subln, diff_w_out):
    batch, seq, _ = x.shape
    depth = norm_g.shape[0]
    n_a = depth // 2
    t = batch * seq
    assert t % TOKEN_TILE == 0 and seq % MLSTM_SEQ_TILE == 0 and seq % ATTN_Q_TILE == 0
    assert TOKEN_TILE % MLSTM_CHUNK == 0 and MLSTM_SEQ_TILE % TOKEN_TILE == 0

    bf = lambda w: w.astype(BF16)
    x = x.reshape(t, D_MODEL)
    p = p.reshape(depth, t, PLE_DIM)
    k_sh = vt_sh = None
    for layer in range(depth):
        g = norm_g[layer]
        ffn1 = (bf(w_ffn_in[layer, 0]), bf(w_ffn_out[layer, 0]))
        ffn2 = (bf(w_ffn_in[layer, 1]), bf(w_ffn_out[layer, 1]))
        ple = (bf(w_ple_gate[layer]), bf(w_ple_proj[layer]))
        if layer < n_a:
            w_in = mlstm_w_in[layer]
            q_end, k_end = MLSTM_QK_W, 2 * MLSTM_QK_W
            v_end, o_end = k_end + D_MODEL, k_end + 2 * D_MODEL
            pad = GATE_ROWS - 2 * MLSTM_HEADS
            wgt = jnp.pad(w_in[:, o_end:].T, ((0, pad), (0, 0)))
            bg = jnp.pad(mlstm_b_gates[layer], (0, pad)).reshape(GATE_ROWS, 1)
            proj = (bf(w_in[:, :q_end]), bf(w_in[:, q_end:k_end].T), bf(w_in[:, k_end:v_end]),
                    bf(w_in[:, v_end:o_end]), bf(wgt), bg)
            x, q, kt, v, o, gates = _pre_mixer(x, g, *ffn1, proj, "mlstm", batch)
            h = _mlstm_core(q, kt, v, o, gates, mlstm_head_norm[layer], batch, seq)
            w_mix = bf(mlstm_w_out[layer])
        else:
            j = layer - n_a
            lam_init = 0.8 - 0.6 * math.exp(-0.3 * layer)
            x, qt = _pre_mixer(x, g, *ffn1, (bf(diff_w_q[j].T),), "attn", batch)
            h = _diff_attn(qt, k_sh.reshape(batch, seq, D_MODEL), vt_sh,
                           diff_lambda[j], diff_subln[j].reshape(DIFF_PAIR_W, 1), lam_init)
            w_mix = bf(diff_w_out[j])
        if layer == n_a - 1:
            x, k_sh, vt_sh = _post_mixer(x, h, p[layer], g, w_mix, *ffn2, *ple,
                                         kv_norm.reshape(1, D_MODEL), bf(w_kv[:, :D_MODEL]),
                                         bf(w_kv[:, D_MODEL:].T), batch)
        else:
            x = _post_mixer(x, h, p[layer], g, w_mix, *ffn2, *ple)
    return x.reshape(batch, seq, D_MODEL)
```

```python
import functools
import math

import jax
import jax.numpy as jnp
from jax import lax
from jax.experimental import pallas as pl
from jax.experimental.pallas import tpu as pltpu

F32 = jnp.float32
BF16 = jnp.bfloat16

D_MODEL = 1024
D_FF = 2816
PLE_DIM = 256
EPS = 1e-6
LOG2E = math.log2(math.e)

MLSTM_HEADS = 4
MLSTM_DV = 256
MLSTM_DK = 128
MLSTM_CHUNK = 128
MLSTM_QK_W = MLSTM_HEADS * MLSTM_DK
GATE_ROWS = 16

DIFF_HEAD_DIM = 64
DIFF_HEADS = 8
DIFF_PAIR_W = 2 * DIFF_HEAD_DIM

VMEM_LIMIT_BYTES = 56 * 1024 * 1024

TOKEN_TILE = 512
FFN_CHUNK = 256
MLSTM_SEQ_TILE = 1024
ATTN_Q_TILE = 512
ATTN_K_TILE = 256
ATTN_SUM_ROWS = 16
ATTN_HEADS_PER_STEP = 2


def _rms(x, g):
    ms = jnp.mean(x * x, axis=-1, keepdims=True)
    return (x * lax.rsqrt(ms + EPS)) * g


def _dot(a, b):
    return jnp.dot(a, b, preferred_element_type=F32)


def _dot_nt(a, b):
    return lax.dot_general(a, b, (((1,), (1,)), ((), ())), preferred_element_type=F32)


def _params(*semantics):
    return pltpu.CompilerParams(dimension_semantics=semantics, vmem_limit_bytes=VMEM_LIMIT_BYTES)


def _resident(w):
    arr, lead = w if isinstance(w, tuple) else (w, ())
    tail = arr.shape[len(lead):]
    index = tuple(lead) + (0,) * len(tail)
    return pl.BlockSpec((None,) * len(lead) + tail, lambda *_: index, pipeline_mode=pl.Buffered(1))


def _operand(w):
    return w[0] if isinstance(w, tuple) else w


def _row_tile(width, tile=TOKEN_TILE):
    return pl.BlockSpec((tile, width), lambda i: (i, 0))


def _feature_major_tile(tiles_per_seq):
    return pl.BlockSpec((1, 1, D_MODEL, TOKEN_TILE), lambda i: (i // tiles_per_seq, i % tiles_per_seq, 0, 0))


def _ffn_math(x, g_ref, win_ref, wout_ref, pre, post):
    xn = _rms(x, g_ref[pre:pre + 1, :]).astype(BF16)
    acc = jnp.zeros(x.shape, F32)
    for c in range(D_FF // FFN_CHUNK):
        lo = c * FFN_CHUNK
        gate = _dot(xn, win_ref[:, lo:lo + FFN_CHUNK])
        up = _dot(xn, win_ref[:, D_FF + lo:D_FF + lo + FFN_CHUNK])
        act = (gate * jax.nn.sigmoid(gate) * up).astype(BF16)
        acc = acc + _dot(act, wout_ref[lo:lo + FFN_CHUNK, :])
    return x + 0.5 * _rms(acc, g_ref[post:post + 1, :])


def _mlstm_proj_math(x, g_ref, wq_ref, wkt_ref, wv_ref, wo_ref, wgt_ref, bg_ref,
                     q_ref, kt_ref, v_ref, o_ref, gates_ref):
    xn = _rms(x, g_ref[2:3, :]).astype(BF16)
    q_ref[...] = _dot(xn, wq_ref[...]).astype(BF16)
    v_ref[...] = _dot(xn, wv_ref[...]).astype(BF16)
    o_ref[...] = _dot(xn, wo_ref[...])
    kt = (_dot_nt(wkt_ref[...], xn) * (MLSTM_DK ** -0.5)).astype(BF16)
    z = _dot_nt(wgt_ref[...], xn) + bg_ref[...]
    row = lax.broadcasted_iota(jnp.int32, z.shape, 0)
    is_forget = (row >= MLSTM_HEADS) & (row < 2 * MLSTM_HEADS)
    gates = jnp.where(is_forget, jax.nn.log_sigmoid(z), z)
    for c in range(TOKEN_TILE // MLSTM_CHUNK):
        sl = slice(c * MLSTM_CHUNK, (c + 1) * MLSTM_CHUNK)
        kt_ref[c] = kt[:, sl]
        gates_ref[c] = gates[:, sl]


def _q_proj_math(x, g_ref, wt_ref, qt_ref):
    xn = _rms(x, g_ref[2:3, :]).astype(BF16)
    qt_ref[0, 0] = (_dot_nt(wt_ref[...], xn) * (DIFF_HEAD_DIM ** -0.5 * LOG2E)).astype(BF16)


def _pre_mixer_body(x_ref, g_ref, win_ref, wout_ref, *rest, mixer):
    *w_refs, y_ref = rest[:-5] if mixer == "mlstm" else rest[:-1]
    out_refs = rest[-5:] if mixer == "mlstm" else rest[-1:]
    y = _ffn_math(x_ref[...], g_ref, win_ref, wout_ref, 0, 1)
    y_ref[...] = y
    if mixer == "mlstm":
        _mlstm_proj_math(y, g_ref, *w_refs, *out_refs)
    else:
        _q_proj_math(y, g_ref, *w_refs, *out_refs)


def _pre_mixer(x, g, w_in, w_out, proj_weights, mixer, batch):
    t = x.shape[0]
    nsb = t // batch // TOKEN_TILE
    if mixer == "mlstm":
        cpt = TOKEN_TILE // MLSTM_CHUNK
        nchunks = t // MLSTM_CHUNK
        proj_specs = [
            _row_tile(MLSTM_QK_W),
            pl.BlockSpec((cpt, MLSTM_QK_W, MLSTM_CHUNK), lambda i: (i, 0, 0)),
            _row_tile(D_MODEL),
            _row_tile(D_MODEL),
            pl.BlockSpec((cpt, GATE_ROWS, MLSTM_CHUNK), lambda i: (i, 0, 0)),
        ]
        proj_shapes = [
            jax.ShapeDtypeStruct((t, MLSTM_QK_W), BF16),
            jax.ShapeDtypeStruct((nchunks, MLSTM_QK_W, MLSTM_CHUNK), BF16),
            jax.ShapeDtypeStruct((t, D_MODEL), BF16),
            jax.ShapeDtypeStruct((t, D_MODEL), F32),
            jax.ShapeDtypeStruct((nchunks, GATE_ROWS, MLSTM_CHUNK), F32),
        ]
    else:
        proj_specs = [_feature_major_tile(nsb)]
        proj_shapes = [jax.ShapeDtypeStruct((batch, nsb, D_MODEL, TOKEN_TILE), BF16)]
    return pl.pallas_call(
        functools.partial(_pre_mixer_body, mixer=mixer),
        grid=(t // TOKEN_TILE,),
        in_specs=[_row_tile(D_MODEL)] + [_resident(w) for w in (g, w_in, w_out, *proj_weights)],
        out_specs=[_row_tile(D_MODEL)] + proj_specs,
        out_shape=[jax.ShapeDtypeStruct(x.shape, F32)] + proj_shapes,
        compiler_params=_params("parallel"),
        name="pre_" + mixer,
    )(x, *[_operand(w) for w in (g, w_in, w_out, *proj_weights)])


def _post_mixer_body(x_ref, h_ref, p_ref, g_ref, wmix_ref, win_ref, wout_ref, wgate_ref, wple_ref,
                     *rest, with_kv, h_feature_major):
    if with_kv:
        kvn_ref, wk_ref, wvt_ref, o_ref, k_ref, vt_ref = rest
    else:
        (o_ref,) = rest
    x = x_ref[...]
    if h_feature_major:
        mixed = lax.dot_general(h_ref[0, 0], wmix_ref[...], (((0,), (0,)), ((), ())),
                                preferred_element_type=F32)
    else:
        mixed = _dot(h_ref[...], wmix_ref[...])
    x = x + _rms(mixed, g_ref[3:4, :])
    x = _ffn_math(x, g_ref, win_ref, wout_ref, 4, 5)
    xn = _rms(x, g_ref[6:7, :]).astype(BF16)
    gate = jax.nn.sigmoid(_dot(xn, wgate_ref[...]))
    e = _dot(p_ref[...].astype(BF16), wple_ref[...])
    y = x + _rms(e * gate, g_ref[7:8, :])
    o_ref[...] = y
    if with_kv:
        yn = _rms(y, kvn_ref[...]).astype(BF16)
        k_ref[...] = _dot(yn, wk_ref[...]).astype(BF16)
        vt_ref[0, 0] = _dot_nt(wvt_ref[...], yn).astype(BF16)


def _post_mixer(x, h, p, g, w_mix, w_in, w_out, w_gate, w_ple, kv_norm=None, w_k=None, w_vt=None, batch=None):
    t = x.shape[0]
    with_kv = w_k is not None
    p_all, layer = p
    p_spec = pl.BlockSpec((None, TOKEN_TILE, PLE_DIM), lambda i: (layer, i, 0))
    h_feature_major = h.ndim == 4
    h_spec = _feature_major_tile(h.shape[1]) if h_feature_major else _row_tile(D_MODEL)
    weights = [g, w_mix, w_in, w_out, w_gate, w_ple]
    out_specs = _row_tile(D_MODEL)
    out_shape = jax.ShapeDtypeStruct(x.shape, F32)
    if with_kv:
        nsb = t // batch // TOKEN_TILE
        weights += [kv_norm, w_k, w_vt]
        out_specs = [out_specs, _row_tile(D_MODEL), _feature_major_tile(nsb)]
        out_shape = [out_shape, jax.ShapeDtypeStruct((t, D_MODEL), BF16),
                     jax.ShapeDtypeStruct((batch, nsb, D_MODEL, TOKEN_TILE), BF16)]
    return pl.pallas_call(
        functools.partial(_post_mixer_body, with_kv=with_kv, h_feature_major=h_feature_major),
        grid=(t // TOKEN_TILE,),
        in_specs=[_row_tile(D_MODEL), h_spec, p_spec] + [_resident(w) for w in weights],
        out_specs=out_specs,
        out_shape=out_shape,
        compiler_params=_params("parallel"),
        name="post_mixer_kv" if with_kv else "post_mixer",
    )(x, h, p_all, *[_operand(w) for w in weights])


def _split3(x):
    hi = x.astype(BF16)
    r = x - hi.astype(F32)
    mid = r.astype(BF16)
    lo = (r - mid.astype(F32)).astype(BF16)
    return hi, mid, lo


def _mlstm_core_body(q_ref, kt_ref, v_ref, o_ref, gates_ref, hn_ref, out_ref,
                     c_ref, m_ref, ext_ref, u_ref, den_ref, mloc_ref, bcol_ref, blast_ref, mg_ref):
    L = MLSTM_CHUNK
    n_chunks = MLSTM_SEQ_TILE // L
    dk, dv = MLSTM_DK, MLSTM_DV

    @pl.when(pl.program_id(1) == 0)
    def _():
        c_ref[...] = jnp.zeros(c_ref.shape, F32)
        m_ref[...] = jnp.full(m_ref.shape, -jnp.inf, F32)

    r_i = lax.broadcasted_iota(jnp.int32, (L, L), 0)
    c_i = lax.broadcasted_iota(jnp.int32, (L, L), 1)
    causal = c_i <= r_i
    upper = (r_i <= c_i).astype(BF16)
    ones_blk = jnp.ones((L, 128), BF16)

    def prepare(c, slot):
        row0 = pl.multiple_of(c * L, L)
        g_all = gates_ref[c]
        b_all = sum(_dot(part, upper) for part in _split3(g_all))
        for h in range(MLSTM_HEADS):
            li = g_all[h:h + 1, :]
            b_row = b_all[MLSTM_HEADS + h:MLSTM_HEADS + h + 1, :]
            q = q_ref[pl.ds(row0, L), h * dk:(h + 1) * dk]
            kt = kt_ref[c, h * dk:(h + 1) * dk, :]
            v = v_ref[pl.ds(row0, L), h * dv:(h + 1) * dv]
            v_ext = jnp.concatenate([v, ones_blk], axis=1)

            b_col = jnp.broadcast_to(b_row, (L, L)).T
            a_row = li - b_row
            d = jnp.where(causal, b_col + a_row, -jnp.inf)
            m_loc = jnp.max(d, axis=-1, keepdims=True)
            qk = _dot(q, kt) * jnp.exp(d - m_loc)
            ext_ref[slot, h] = _dot(qk.astype(BF16), v)
            den_ref[slot, h] = jnp.sum(qk, axis=-1, keepdims=True)
            mloc_ref[slot, h] = m_loc
            bcol_ref[slot, h] = b_col[:, 0:1]

            b_last = b_row[:, L - 1:L]
            g_row = b_last + a_row
            mg = jnp.max(g_row, axis=-1, keepdims=True)
            kts = (kt.astype(F32) * jnp.exp(g_row - mg)).astype(BF16)
            u_ref[slot, h] = _dot(kts, v_ext)
            blast_ref[slot, h] = jnp.broadcast_to(b_last, (1, 128))
            mg_ref[slot, h] = jnp.broadcast_to(mg, (1, 128))

    def advance(c, slot):
        row0 = pl.multiple_of(c * L, L)
        for h in range(MLSTM_HEADS):
            m_prev = m_ref[h:h + 1, 0:1]
            q = q_ref[pl.ds(row0, L), h * dk:(h + 1) * dk]
            m_loc = mloc_ref[slot, h]
            inter = bcol_ref[slot, h] + m_prev
            m_t = jnp.maximum(inter, m_loc)
            r_loc = jnp.exp(m_loc - m_t)
            s_inter = jnp.exp(inter - m_t)
            carried = _dot(q, c_ref[h].astype(BF16))
            num = ext_ref[slot, h] * r_loc + s_inter * carried[:, :dv]
            den = den_ref[slot, h] * r_loc + s_inter * carried[:, dv:dv + 1]
            hid = num / jnp.maximum(jnp.abs(den), jnp.exp(-m_t))

            hn = _rms(hid, hn_ref[h:h + 1, :])
            og = o_ref[pl.ds(row0, L), h * dv:(h + 1) * dv]
            out_ref[pl.ds(row0, L), h * dv:(h + 1) * dv] = (jax.nn.sigmoid(og) * hn).astype(BF16)

            b_last = blast_ref[slot, h][:, 0:1]
            mg = mg_ref[slot, h][:, 0:1]
            m_new = jnp.maximum(b_last + m_prev, mg)
            c_ref[h] = jnp.exp(b_last + m_prev - m_new) * c_ref[h] + jnp.exp(mg - m_new) * u_ref[slot, h]
            m_ref[h:h + 1, :] = jnp.broadcast_to(m_new, (1, 128))

    prepare(0, 0)

    def two_chunks(i, carry):
        c = 2 * i
        prepare(c + 1, 1)
        advance(c, 0)
        prepare(c + 2, 0)
        advance(c + 1, 1)
        return carry

    lax.fori_loop(0, n_chunks // 2 - 1, two_chunks, 0)
    prepare(n_chunks - 1, 1)
    advance(n_chunks - 2, 0)
    advance(n_chunks - 1, 1)


def _mlstm_core(q, kt, v, o, gates, head_norm, batch, seq):
    t = q.shape[0]
    ns = seq // MLSTM_SEQ_TILE
    cps = MLSTM_SEQ_TILE // MLSTM_CHUNK

    def rows(width):
        return pl.BlockSpec((MLSTM_SEQ_TILE, width), lambda b, s: (b * ns + s, 0))

    def slabs(height):
        return pl.BlockSpec((cps, height, MLSTM_CHUNK), lambda b, s: (b * ns + s, 0, 0))

    assert cps % 2 == 0 and cps >= 4
    col_stat = pltpu.VMEM((2, MLSTM_HEADS, MLSTM_CHUNK, 1), F32)
    scalar_stat = pltpu.VMEM((2, MLSTM_HEADS, 1, 128), F32)

    return pl.pallas_call(
        _mlstm_core_body,
        grid=(batch, ns),
        in_specs=[rows(MLSTM_QK_W), slabs(MLSTM_QK_W), rows(D_MODEL), rows(D_MODEL), slabs(GATE_ROWS),
                  pl.BlockSpec(head_norm.shape, lambda b, s: (0, 0))],
        out_specs=rows(D_MODEL),
        out_shape=jax.ShapeDtypeStruct((t, D_MODEL), BF16),
        scratch_shapes=[pltpu.VMEM((MLSTM_HEADS, MLSTM_DK, MLSTM_DV + 128), F32),
                        pltpu.VMEM((8, 128), F32),
                        pltpu.VMEM((2, MLSTM_HEADS, MLSTM_CHUNK, MLSTM_DV), F32),
                        pltpu.VMEM((2, MLSTM_HEADS, MLSTM_DK, MLSTM_DV + 128), F32),
                        col_stat, col_stat, col_stat, scalar_stat, scalar_stat],
        compiler_params=_params("parallel", "arbitrary"),
        name="mlstm_core",
    )(q, kt, v, o, gates, head_norm)


def _attn_body(qt_ref, k_ref, vt_ref, lam_ref, subln_ref, o_ref,
               sa_ref, sb_ref, bma_ref, bmb_ref, m_ref, acc_ref, *, lam_init):
    lax.fori_loop(0, qt_ref.shape[1], functools.partial(
        _attn_query_tile, refs=(qt_ref, k_ref, vt_ref, lam_ref, subln_ref, o_ref,
                                sa_ref, sb_ref, bma_ref, bmb_ref, m_ref, acc_ref), lam_init=lam_init), 0)


def _attn_query_tile(qi, carry, *, refs, lam_init):
    qt_ref, k_ref, vt_ref, lam_ref, subln_ref, o_ref, sa_ref, sb_ref, bma_ref, bmb_ref, m_ref, acc_ref = refs
    tq, tk = ATTN_Q_TILE, ATTN_K_TILE
    dv = DIFF_PAIR_W
    n_chains = 2 * ATTN_HEADS_PER_STEP
    qt = qt_ref[0, qi]
    row = lax.broadcasted_iota(jnp.int32, qt.shape, 0)
    zero = jnp.zeros_like(qt)
    qts = [jnp.where((row >= c * DIFF_HEAD_DIM) & (row < (c + 1) * DIFF_HEAD_DIM), qt, zero)
           for c in range(n_chains)]
    acc_ref[...] = jnp.zeros(acc_ref.shape, F32)
    m_ref[...] = jnp.full(m_ref.shape, -jnp.inf, F32)

    def score(pair, half, s_ref, bm_ref, mask_offset, q0=0):
        k0 = pl.multiple_of(pair * tq + half * tk, tk)
        k = k_ref[0, pl.ds(k0, tk), :]
        for c, qm in enumerate(qts):
            s = _dot(k, qm[:, q0:])
            if mask_offset is not None:
                key = lax.broadcasted_iota(jnp.int32, s.shape, 0) + mask_offset
                qry = lax.broadcasted_iota(jnp.int32, s.shape, 1) + q0
                s = jnp.where(key <= qry, s, -jnp.inf)
            s_ref[c, :, q0:] = s
            bm_ref[c, :, q0:] = jnp.max(s, axis=0, keepdims=True)

    ones_rows = jnp.ones((ATTN_SUM_ROWS, tk), BF16)

    def absorb(pair, half, s_ref, bm_ref, q0=0):
        for c in range(n_chains):
            head = c // 2
            vt = vt_ref[0, pair, head * dv:(head + 1) * dv, half * tk:(half + 1) * tk]
            vt_ext = jnp.concatenate([vt, ones_rows], axis=0)
            m = m_ref[c, :, q0:]
            m_new = jnp.maximum(m, bm_ref[c, :, q0:])
            alpha = jnp.exp2(m - m_new)
            p = jnp.exp2(s_ref[c, :, q0:] - m_new)
            acc_ref[c, :, q0:] = alpha * acc_ref[c, :, q0:] + _dot(vt_ext, p.astype(BF16))
            m_ref[c, :, q0:] = m_new

    def pair_step(pair, next_is_diagonal):
        score(pair, 1, sb_ref, bmb_ref, None)
        absorb(pair, 0, sa_ref, bma_ref)
        score(pair + 1, 0, sa_ref, bma_ref, 0 if next_is_diagonal else None)
        absorb(pair, 1, sb_ref, bmb_ref)

    def diagonal_tail():
        score(qi, 1, sb_ref, bmb_ref, tk, q0=tk)
        absorb(qi, 0, sa_ref, bma_ref)
        absorb(qi, 1, sb_ref, bmb_ref, q0=tk)

    @pl.when(qi == 0)
    def _():
        score(0, 0, sa_ref, bma_ref, 0)
        diagonal_tail()

    @pl.when(qi > 0)
    def _():
        score(0, 0, sa_ref, bma_ref, None)
        n_plain = qi - 1

        def body(i, carry):
            pair_step(2 * i, False)
            pair_step(2 * i + 1, False)
            return carry

        lax.fori_loop(0, lax.shift_right_logical(n_plain, 1), body, 0)

        @pl.when((n_plain & 1) == 1)
        def _():
            pair_step(n_plain - 1, False)

        pair_step(qi - 1, True)
        diagonal_tail()

    lv = lam_ref[...]
    lam = (jnp.exp(jnp.sum(lv[0:1, :] * lv[1:2, :], axis=-1, keepdims=True))
           - jnp.exp(jnp.sum(lv[2:3, :] * lv[3:4, :], axis=-1, keepdims=True)) + lam_init)
    for head in range(ATTN_HEADS_PER_STEP):
        c1, c2 = 2 * head, 2 * head + 1
        ot = (acc_ref[c1, :dv, :] / acc_ref[c1, dv:dv + 1, :]
              - lam * (acc_ref[c2, :dv, :] / acc_ref[c2, dv:dv + 1, :]))
        ms = jnp.mean(ot * ot, axis=0, keepdims=True)
        o_ref[0, qi, head * dv:(head + 1) * dv, :] = (
            ((ot * lax.rsqrt(ms + EPS)) * subln_ref[...]) * (1.0 - lam_init)).astype(BF16)
    return carry


def _diff_attn(qt, k, vt, lam_vecs, subln, lam_init):
    assert ATTN_Q_TILE == TOKEN_TILE and ATTN_Q_TILE == 2 * ATTN_K_TILE
    batch, seq, _ = k.shape
    nqb = seq // ATTN_Q_TILE
    n_chains = 2 * ATTN_HEADS_PER_STEP
    width = ATTN_HEADS_PER_STEP * DIFF_PAIR_W
    stat = pltpu.VMEM((n_chains, 1, ATTN_Q_TILE), F32)
    scores = pltpu.VMEM((n_chains, ATTN_K_TILE, ATTN_Q_TILE), F32)
    return pl.pallas_call(
        functools.partial(_attn_body, lam_init=lam_init),
        grid=(batch, DIFF_HEADS // ATTN_HEADS_PER_STEP),
        in_specs=[
            pl.BlockSpec((1, nqb, width, ATTN_Q_TILE), lambda b, h: (b, 0, h, 0)),
            pl.BlockSpec((1, seq, width), lambda b, h: (b, 0, h)),
            pl.BlockSpec((1, nqb, width, ATTN_Q_TILE), lambda b, h: (b, 0, h, 0)),
            pl.BlockSpec(lam_vecs.shape, lambda b, h: (0, 0)),
            pl.BlockSpec(subln.shape, lambda b, h: (0, 0)),
        ],
        out_specs=pl.BlockSpec((1, nqb, width, ATTN_Q_TILE), lambda b, h: (b, 0, h, 0)),
        out_shape=jax.ShapeDtypeStruct((batch, nqb, D_MODEL, ATTN_Q_TILE), BF16),
        scratch_shapes=[scores, scores, stat, stat, stat,
                        pltpu.VMEM((n_chains, DIFF_PAIR_W + ATTN_SUM_ROWS, ATTN_Q_TILE), F32)],
        compiler_params=_params("parallel", "parallel"),
        name="diff_attn",
    )(qt, k, vt, lam_vecs, subln)


def kernel(x, p, norm_g, w_ffn_in, w_ffn_out, w_ple_proj, w_ple_gate, mlstm_w_in, mlstm_b_gates,
           mlstm_head_norm, mlstm_w_out, kv_norm, w_kv, diff_w_q, diff_lambda, diff_subln, diff_w_out):
    batch, seq, _ = x.shape
    depth = norm_g.shape[0]
    n_a = depth // 2
    t = batch * seq
    assert t % TOKEN_TILE == 0 and seq % MLSTM_SEQ_TILE == 0 and seq % ATTN_Q_TILE == 0
    assert TOKEN_TILE % MLSTM_CHUNK == 0 and MLSTM_SEQ_TILE % TOKEN_TILE == 0

    bf = lambda w: w.astype(BF16)
    x = x.reshape(t, D_MODEL)
    p = p.reshape(depth, t, PLE_DIM)
    w_ffn_in, w_ffn_out, w_ple_gate, w_ple_proj = bf(w_ffn_in), bf(w_ffn_out), bf(w_ple_gate), bf(w_ple_proj)
    k_sh = vt_sh = None
    for layer in range(depth):
        g = (norm_g, (layer,))
        ffn1 = ((w_ffn_in, (layer, 0)), (w_ffn_out, (layer, 0)))
        ffn2 = ((w_ffn_in, (layer, 1)), (w_ffn_out, (layer, 1)))
        ple = ((w_ple_gate, (layer,)), (w_ple_proj, (layer,)))
        if layer < n_a:
            w_in = mlstm_w_in[layer]
            q_end, k_end = MLSTM_QK_W, 2 * MLSTM_QK_W
            v_end, o_end = k_end + D_MODEL, k_end + 2 * D_MODEL
            pad = GATE_ROWS - 2 * MLSTM_HEADS
            wgt = jnp.pad(w_in[:, o_end:].T, ((0, pad), (0, 0)))
            bg = jnp.pad(mlstm_b_gates[layer], (0, pad)).reshape(GATE_ROWS, 1)
            proj = (bf(w_in[:, :q_end]), bf(w_in[:, q_end:k_end].T), bf(w_in[:, k_end:v_end]),
                    bf(w_in[:, v_end:o_end]), bf(wgt), bg)
            x, q, kt, v, o, gates = _pre_mixer(x, g, *ffn1, proj, "mlstm", batch)
            h = _mlstm_core(q, kt, v, o, gates, mlstm_head_norm[layer], batch, seq)
            w_mix = bf(mlstm_w_out[layer])
        else:
            j = layer - n_a
            lam_init = 0.8 - 0.6 * math.exp(-0.3 * layer)
            x, qt = _pre_mixer(x, g, *ffn1, (bf(diff_w_q[j].T),), "attn", batch)
            h = _diff_attn(qt, k_sh.reshape(batch, seq, D_MODEL), vt_sh,
                           diff_lambda[j], diff_subln[j].reshape(DIFF_PAIR_W, 1), lam_init)
            w_mix = bf(diff_w_out[j])
        if layer == n_a - 1:
            x, k_sh, vt_sh = _post_mixer(x, h, (p, layer), g, w_mix, *ffn2, *ple,
                                         kv_norm.reshape(1, D_MODEL), bf(w_kv[:, :D_MODEL]),
                                         bf(w_kv[:, D_MODEL:].T), batch)
        else:
            x = _post_mixer(x, h, (p, layer), g, w_mix, *ffn2, *ple)
    return x.reshape(batch, seq, D_MODEL)
```

```python
import functools
import math

import jax
import jax.numpy as jnp
from jax import lax
from jax.experimental import pallas as pl
from jax.experimental.pallas import tpu as pltpu

F32 = jnp.float32
BF16 = jnp.bfloat16

D_MODEL = 1024
D_FF = 2816
PLE_DIM = 256
EPS = 1e-6
LOG2E = math.log2(math.e)

MLSTM_HEADS = 4
MLSTM_DV = 256
MLSTM_DK = 128
MLSTM_CHUNK = 128
MLSTM_QK_W = MLSTM_HEADS * MLSTM_DK
GATE_ROWS = 16

DIFF_HEAD_DIM = 64
DIFF_HEADS = 8
DIFF_PAIR_W = 2 * DIFF_HEAD_DIM

VMEM_LIMIT_BYTES = 56 * 1024 * 1024

TOKEN_TILE = 512
FFN_CHUNK = 256
MLSTM_SEQ_TILE = 2048
ATTN_Q_TILE = 512
ATTN_K_TILE = 256
ATTN_SUM_ROWS = 16
ATTN_HEADS_PER_STEP = 2


def _rms(x, g):
    ms = jnp.mean(x * x, axis=-1, keepdims=True)
    return (x * lax.rsqrt(ms + EPS)) * g


def _dot(a, b):
    return jnp.dot(a, b, preferred_element_type=F32)


def _dot_nt(a, b):
    return lax.dot_general(a, b, (((1,), (1,)), ((), ())), preferred_element_type=F32)


def _params(*semantics):
    return pltpu.CompilerParams(dimension_semantics=semantics, vmem_limit_bytes=VMEM_LIMIT_BYTES)


def _resident(w):
    arr, lead = w if isinstance(w, tuple) else (w, ())
    tail = arr.shape[len(lead):]
    index = tuple(lead) + (0,) * len(tail)
    return pl.BlockSpec((None,) * len(lead) + tail, lambda *_: index, pipeline_mode=pl.Buffered(1))


def _operand(w):
    return w[0] if isinstance(w, tuple) else w


def _row_tile(width, tile=TOKEN_TILE):
    return pl.BlockSpec((tile, width), lambda i: (i, 0))


def _feature_major_tile(tiles_per_seq):
    return pl.BlockSpec((1, 1, D_MODEL, TOKEN_TILE), lambda i: (i // tiles_per_seq, i % tiles_per_seq, 0, 0))


def _ffn_math(x, g_ref, win_ref, wout_ref, pre, post):
    xn = _rms(x, g_ref[pre:pre + 1, :]).astype(BF16)
    acc = jnp.zeros(x.shape, F32)
    for c in range(D_FF // FFN_CHUNK):
        lo = c * FFN_CHUNK
        gate = _dot(xn, win_ref[:, lo:lo + FFN_CHUNK])
        up = _dot(xn, win_ref[:, D_FF + lo:D_FF + lo + FFN_CHUNK])
        act = (gate * jax.nn.sigmoid(gate) * up).astype(BF16)
        acc = acc + _dot(act, wout_ref[lo:lo + FFN_CHUNK, :])
    return x + 0.5 * _rms(acc, g_ref[post:post + 1, :])


def _mlstm_proj_math(x, g_ref, wq_ref, wkt_ref, wv_ref, wo_ref, wgt_ref, bg_ref,
                     q_ref, kt_ref, v_ref, o_ref, gates_ref):
    xn = _rms(x, g_ref[2:3, :]).astype(BF16)
    q_ref[...] = _dot(xn, wq_ref[...]).astype(BF16)
    v_ref[...] = _dot(xn, wv_ref[...]).astype(BF16)
    o_ref[...] = _dot(xn, wo_ref[...])
    kt = (_dot_nt(wkt_ref[...], xn) * (MLSTM_DK ** -0.5)).astype(BF16)
    z = _dot_nt(wgt_ref[...], xn) + bg_ref[...]
    row = lax.broadcasted_iota(jnp.int32, z.shape, 0)
    is_forget = (row >= MLSTM_HEADS) & (row < 2 * MLSTM_HEADS)
    gates = jnp.where(is_forget, jax.nn.log_sigmoid(z), z)
    for c in range(TOKEN_TILE // MLSTM_CHUNK):
        sl = slice(c * MLSTM_CHUNK, (c + 1) * MLSTM_CHUNK)
        kt_ref[c] = kt[:, sl]
        gates_ref[c] = gates[:, sl]


def _q_proj_math(x, g_ref, wt_ref, qt_ref):
    xn = _rms(x, g_ref[2:3, :]).astype(BF16)
    qt_ref[0, 0] = (_dot_nt(wt_ref[...], xn) * (DIFF_HEAD_DIM ** -0.5 * LOG2E)).astype(BF16)


def _pre_mixer_body(x_ref, g_ref, win_ref, wout_ref, *rest, mixer):
    *w_refs, y_ref = rest[:-5] if mixer == "mlstm" else rest[:-1]
    out_refs = rest[-5:] if mixer == "mlstm" else rest[-1:]
    y = _ffn_math(x_ref[...], g_ref, win_ref, wout_ref, 0, 1)
    y_ref[...] = y
    if mixer == "mlstm":
        _mlstm_proj_math(y, g_ref, *w_refs, *out_refs)
    else:
        _q_proj_math(y, g_ref, *w_refs, *out_refs)


def _pre_mixer(x, g, w_in, w_out, proj_weights, mixer, batch):
    t = x.shape[0]
    nsb = t // batch // TOKEN_TILE
    if mixer == "mlstm":
        cpt = TOKEN_TILE // MLSTM_CHUNK
        nchunks = t // MLSTM_CHUNK
        proj_specs = [
            _row_tile(MLSTM_QK_W),
            pl.BlockSpec((cpt, MLSTM_QK_W, MLSTM_CHUNK), lambda i: (i, 0, 0)),
            _row_tile(D_MODEL),
            _row_tile(D_MODEL),
            pl.BlockSpec((cpt, GATE_ROWS, MLSTM_CHUNK), lambda i: (i, 0, 0)),
        ]
        proj_shapes = [
            jax.ShapeDtypeStruct((t, MLSTM_QK_W), BF16),
            jax.ShapeDtypeStruct((nchunks, MLSTM_QK_W, MLSTM_CHUNK), BF16),
            jax.ShapeDtypeStruct((t, D_MODEL), BF16),
            jax.ShapeDtypeStruct((t, D_MODEL), F32),
            jax.ShapeDtypeStruct((nchunks, GATE_ROWS, MLSTM_CHUNK), F32),
        ]
    else:
        proj_specs = [_feature_major_tile(nsb)]
        proj_shapes = [jax.ShapeDtypeStruct((batch, nsb, D_MODEL, TOKEN_TILE), BF16)]
    return pl.pallas_call(
        functools.partial(_pre_mixer_body, mixer=mixer),
        grid=(t // TOKEN_TILE,),
        in_specs=[_row_tile(D_MODEL)] + [_resident(w) for w in (g, w_in, w_out, *proj_weights)],
        out_specs=[_row_tile(D_MODEL)] + proj_specs,
        out_shape=[jax.ShapeDtypeStruct(x.shape, F32)] + proj_shapes,
        compiler_params=_params("parallel"),
        name="pre_" + mixer,
    )(x, *[_operand(w) for w in (g, w_in, w_out, *proj_weights)])


def _post_mixer_body(x_ref, h_ref, p_ref, g_ref, wmix_ref, win_ref, wout_ref, wgate_ref, wple_ref,
                     *rest, with_kv, h_feature_major):
    if with_kv:
        kvn_ref, wk_ref, wvt_ref, o_ref, k_ref, vt_ref = rest
    else:
        (o_ref,) = rest
    x = x_ref[...]
    if h_feature_major:
        mixed = lax.dot_general(h_ref[0, 0], wmix_ref[...], (((0,), (0,)), ((), ())),
                                preferred_element_type=F32)
    else:
        mixed = _dot(h_ref[...], wmix_ref[...])
    x = x + _rms(mixed, g_ref[3:4, :])
    x = _ffn_math(x, g_ref, win_ref, wout_ref, 4, 5)
    xn = _rms(x, g_ref[6:7, :]).astype(BF16)
    gate = jax.nn.sigmoid(_dot(xn, wgate_ref[...]))
    e = _dot(p_ref[...].astype(BF16), wple_ref[...])
    y = x + _rms(e * gate, g_ref[7:8, :])
    o_ref[...] = y
    if with_kv:
        yn = _rms(y, kvn_ref[...]).astype(BF16)
        k_ref[...] = _dot(yn, wk_ref[...]).astype(BF16)
        vt_ref[0, 0] = _dot_nt(wvt_ref[...], yn).astype(BF16)


def _post_mixer(x, h, p, g, w_mix, w_in, w_out, w_gate, w_ple, kv_norm=None, w_k=None, w_vt=None, batch=None):
    t = x.shape[0]
    with_kv = w_k is not None
    p_all, layer = p
    p_spec = pl.BlockSpec((None, TOKEN_TILE, PLE_DIM), lambda i: (layer, i, 0))
    h_feature_major = h.ndim == 4
    h_spec = _feature_major_tile(h.shape[1]) if h_feature_major else _row_tile(D_MODEL)
    weights = [g, w_mix, w_in, w_out, w_gate, w_ple]
    out_specs = _row_tile(D_MODEL)
    out_shape = jax.ShapeDtypeStruct(x.shape, F32)
    if with_kv:
        nsb = t // batch // TOKEN_TILE
        weights += [kv_norm, w_k, w_vt]
        out_specs = [out_specs, _row_tile(D_MODEL), _feature_major_tile(nsb)]
        out_shape = [out_shape, jax.ShapeDtypeStruct((t, D_MODEL), BF16),
                     jax.ShapeDtypeStruct((batch, nsb, D_MODEL, TOKEN_TILE), BF16)]
    return pl.pallas_call(
        functools.partial(_post_mixer_body, with_kv=with_kv, h_feature_major=h_feature_major),
        grid=(t // TOKEN_TILE,),
        in_specs=[_row_tile(D_MODEL), h_spec, p_spec] + [_resident(w) for w in weights],
        out_specs=out_specs,
        out_shape=out_shape,
        compiler_params=_params("parallel"),
        name="post_mixer_kv" if with_kv else "post_mixer",
    )(x, h, p_all, *[_operand(w) for w in weights])


def _split3(x):
    hi = x.astype(BF16)
    r = x - hi.astype(F32)
    mid = r.astype(BF16)
    lo = (r - mid.astype(F32)).astype(BF16)
    return hi, mid, lo


def _mlstm_core_body(q_ref, kt_ref, v_ref, o_ref, gates_ref, hn_ref, out_ref,
                     c_ref, m_ref, ext_ref, u_ref, den_ref, mloc_ref, bcol_ref, blast_ref, mg_ref):
    L = MLSTM_CHUNK
    n_chunks = MLSTM_SEQ_TILE // L
    dk, dv = MLSTM_DK, MLSTM_DV

    @pl.when(pl.program_id(1) == 0)
    def _():
        c_ref[...] = jnp.zeros(c_ref.shape, F32)
        m_ref[...] = jnp.full(m_ref.shape, -jnp.inf, F32)

    r_i = lax.broadcasted_iota(jnp.int32, (L, L), 0)
    c_i = lax.broadcasted_iota(jnp.int32, (L, L), 1)
    causal = c_i <= r_i
    upper = (r_i <= c_i).astype(BF16)
    ones_blk = jnp.ones((L, 128), BF16)

    def prepare(c, slot):
        row0 = pl.multiple_of(c * L, L)
        g_all = gates_ref[c]
        b_all = sum(_dot(part, upper) for part in _split3(g_all))
        for h in range(MLSTM_HEADS):
            li = g_all[h:h + 1, :]
            b_row = b_all[MLSTM_HEADS + h:MLSTM_HEADS + h + 1, :]
            q = q_ref[pl.ds(row0, L), h * dk:(h + 1) * dk]
            kt = kt_ref[c, h * dk:(h + 1) * dk, :]
            v = v_ref[pl.ds(row0, L), h * dv:(h + 1) * dv]
            v_ext = jnp.concatenate([v, ones_blk], axis=1)

            b_col = jnp.broadcast_to(b_row, (L, L)).T
            a_row = li - b_row
            d = jnp.where(causal, b_col + a_row, -jnp.inf)
            m_loc = jnp.max(d, axis=-1, keepdims=True)
            qk = _dot(q, kt) * jnp.exp(d - m_loc)
            ext_ref[slot, h] = _dot(qk.astype(BF16), v)
            den_ref[slot, h] = jnp.sum(qk, axis=-1, keepdims=True)
            mloc_ref[slot, h] = m_loc
            bcol_ref[slot, h] = b_col[:, 0:1]

            b_last = b_row[:, L - 1:L]
            g_row = b_last + a_row
            mg = jnp.max(g_row, axis=-1, keepdims=True)
            kts = (kt.astype(F32) * jnp.exp(g_row - mg)).astype(BF16)
            u_ref[slot, h] = _dot(kts, v_ext)
            blast_ref[slot, h] = jnp.broadcast_to(b_last, (1, 128))
            mg_ref[slot, h] = jnp.broadcast_to(mg, (1, 128))

    def advance(c, slot):
        row0 = pl.multiple_of(c * L, L)
        for h in range(MLSTM_HEADS):
            m_prev = m_ref[h:h + 1, 0:1]
            q = q_ref[pl.ds(row0, L), h * dk:(h + 1) * dk]
            m_loc = mloc_ref[slot, h]
            inter = bcol_ref[slot, h] + m_prev
            m_t = jnp.maximum(inter, m_loc)
            r_loc = jnp.exp(m_loc - m_t)
            s_inter = jnp.exp(inter - m_t)
            carried = _dot(q, c_ref[h].astype(BF16))
            num = ext_ref[slot, h] * r_loc + s_inter * carried[:, :dv]
            den = den_ref[slot, h] * r_loc + s_inter * carried[:, dv:dv + 1]
            hid = num / jnp.maximum(jnp.abs(den), jnp.exp(-m_t))

            hn = _rms(hid, hn_ref[h:h + 1, :])
            og = o_ref[pl.ds(row0, L), h * dv:(h + 1) * dv]
            out_ref[pl.ds(row0, L), h * dv:(h + 1) * dv] = (jax.nn.sigmoid(og) * hn).astype(BF16)

            b_last = blast_ref[slot, h][:, 0:1]
            mg = mg_ref[slot, h][:, 0:1]
            m_new = jnp.maximum(b_last + m_prev, mg)
            c_ref[h] = jnp.exp(b_last + m_prev - m_new) * c_ref[h] + jnp.exp(mg - m_new) * u_ref[slot, h]
            m_ref[h:h + 1, :] = jnp.broadcast_to(m_new, (1, 128))

    prepare(0, 0)

    def two_chunks(i, carry):
        c = 2 * i
        prepare(c + 1, 1)
        advance(c, 0)
        prepare(c + 2, 0)
        advance(c + 1, 1)
        return carry

    lax.fori_loop(0, n_chunks // 2 - 1, two_chunks, 0)
    prepare(n_chunks - 1, 1)
    advance(n_chunks - 2, 0)
    advance(n_chunks - 1, 1)


def _mlstm_core(q, kt, v, o, gates, head_norm, batch, seq):
    t = q.shape[0]
    ns = seq // MLSTM_SEQ_TILE
    cps = MLSTM_SEQ_TILE // MLSTM_CHUNK

    def rows(width):
        return pl.BlockSpec((MLSTM_SEQ_TILE, width), lambda b, s: (b * ns + s, 0))

    def slabs(height):
        return pl.BlockSpec((cps, height, MLSTM_CHUNK), lambda b, s: (b * ns + s, 0, 0))

    assert cps % 2 == 0 and cps >= 4
    col_stat = pltpu.VMEM((2, MLSTM_HEADS, MLSTM_CHUNK, 1), F32)
    scalar_stat = pltpu.VMEM((2, MLSTM_HEADS, 1, 128), F32)

    return pl.pallas_call(
        _mlstm_core_body,
        grid=(batch, ns),
        in_specs=[rows(MLSTM_QK_W), slabs(MLSTM_QK_W), rows(D_MODEL), rows(D_MODEL), slabs(GATE_ROWS),
                  pl.BlockSpec(head_norm.shape, lambda b, s: (0, 0))],
        out_specs=rows(D_MODEL),
        out_shape=jax.ShapeDtypeStruct((t, D_MODEL), BF16),
        scratch_shapes=[pltpu.VMEM((MLSTM_HEADS, MLSTM_DK, MLSTM_DV + 128), F32),
                        pltpu.VMEM((8, 128), F32),
                        pltpu.VMEM((2, MLSTM_HEADS, MLSTM_CHUNK, MLSTM_DV), F32),
                        pltpu.VMEM((2, MLSTM_HEADS, MLSTM_DK, MLSTM_DV + 128), F32),
                        col_stat, col_stat, col_stat, scalar_stat, scalar_stat],
        compiler_params=_params("parallel", "arbitrary"),
        name="mlstm_core",
    )(q, kt, v, o, gates, head_norm)


def _attn_body(qt_ref, k_ref, vt_ref, lam_ref, subln_ref, o_ref,
               sa_ref, sb_ref, bma_ref, bmb_ref, m_ref, acc_ref, *, lam_init):
    refs = (qt_ref, k_ref, vt_ref, lam_ref, subln_ref, o_ref, sa_ref, sb_ref, bma_ref, bmb_ref, m_ref, acc_ref)
    n_tiles = qt_ref.shape[1]
    _attn_query_tile(jnp.int32(0), refs, lam_init, first=True)

    def later_tile(qi, carry):
        _attn_query_tile(qi, refs, lam_init, first=False)
        return carry

    lax.fori_loop(1, n_tiles, later_tile, 0)
    _attn_write_tile(n_tiles - 1, refs, lam_init)


def _attn_write_tile(qi, refs, lam_init):
    _, _, _, lam_ref, subln_ref, o_ref, _, _, _, _, _, acc_ref = refs
    dv = DIFF_PAIR_W
    lv = lam_ref[...]
    lam = (jnp.exp(jnp.sum(lv[0:1, :] * lv[1:2, :], axis=-1, keepdims=True))
           - jnp.exp(jnp.sum(lv[2:3, :] * lv[3:4, :], axis=-1, keepdims=True)) + lam_init)
    for head in range(ATTN_HEADS_PER_STEP):
        c1, c2 = 2 * head, 2 * head + 1
        ot = (acc_ref[c1, :dv, :] / acc_ref[c1, dv:dv + 1, :]
              - lam * (acc_ref[c2, :dv, :] / acc_ref[c2, dv:dv + 1, :]))
        ms = jnp.mean(ot * ot, axis=0, keepdims=True)
        o_ref[0, qi, head * dv:(head + 1) * dv, :] = (
            ((ot * lax.rsqrt(ms + EPS)) * subln_ref[...]) * (1.0 - lam_init)).astype(BF16)


def _attn_query_tile(qi, refs, lam_init, *, first):
    qt_ref, k_ref, vt_ref, lam_ref, subln_ref, o_ref, sa_ref, sb_ref, bma_ref, bmb_ref, m_ref, acc_ref = refs
    tq, tk = ATTN_Q_TILE, ATTN_K_TILE
    dv = DIFF_PAIR_W
    n_chains = 2 * ATTN_HEADS_PER_STEP
    if not first:
        _attn_write_tile(qi - 1, refs, lam_init)
    qt = qt_ref[0, qi]
    row = lax.broadcasted_iota(jnp.int32, qt.shape, 0)
    zero = jnp.zeros_like(qt)
    qts = [jnp.where((row >= c * DIFF_HEAD_DIM) & (row < (c + 1) * DIFF_HEAD_DIM), qt, zero)
           for c in range(n_chains)]
    acc_ref[...] = jnp.zeros(acc_ref.shape, F32)
    m_ref[...] = jnp.full(m_ref.shape, -jnp.inf, F32)

    def score(pair, half, s_ref, bm_ref, mask_offset, q0=0):
        k0 = pl.multiple_of(pair * tq + half * tk, tk)
        k = k_ref[0, pl.ds(k0, tk), :]
        for c, qm in enumerate(qts):
            s = _dot(k, qm[:, q0:])
            if mask_offset is not None:
                key = lax.broadcasted_iota(jnp.int32, s.shape, 0) + mask_offset
                qry = lax.broadcasted_iota(jnp.int32, s.shape, 1) + q0
                s = jnp.where(key <= qry, s, -jnp.inf)
            s_ref[c, :, q0:] = s
            bm_ref[c, :, q0:] = jnp.max(s, axis=0, keepdims=True)

    ones_rows = jnp.ones((ATTN_SUM_ROWS, tk), BF16)

    def absorb(pair, half, s_ref, bm_ref, q0=0):
        for c in range(n_chains):
            head = c // 2
            vt = vt_ref[0, pair, head * dv:(head + 1) * dv, half * tk:(half + 1) * tk]
            vt_ext = jnp.concatenate([vt, ones_rows], axis=0)
            m = m_ref[c, :, q0:]
            m_new = jnp.maximum(m, bm_ref[c, :, q0:])
            alpha = jnp.exp2(m - m_new)
            p = jnp.exp2(s_ref[c, :, q0:] - m_new)
            acc_ref[c, :, q0:] = alpha * acc_ref[c, :, q0:] + _dot(vt_ext, p.astype(BF16))
            m_ref[c, :, q0:] = m_new

    def pair_step(pair, next_is_diagonal):
        score(pair, 1, sb_ref, bmb_ref, None)
        absorb(pair, 0, sa_ref, bma_ref)
        score(pair + 1, 0, sa_ref, bma_ref, 0 if next_is_diagonal else None)
        absorb(pair, 1, sb_ref, bmb_ref)

    def diagonal_tail():
        score(qi, 1, sb_ref, bmb_ref, tk, q0=tk)
        absorb(qi, 0, sa_ref, bma_ref)
        absorb(qi, 1, sb_ref, bmb_ref, q0=tk)

    if first:
        score(0, 0, sa_ref, bma_ref, 0)
        diagonal_tail()
        return

    score(0, 0, sa_ref, bma_ref, None)
    n_plain = qi - 1

    def body(i, carry):
        pair_step(2 * i, False)
        pair_step(2 * i + 1, False)
        return carry

    lax.fori_loop(0, lax.shift_right_logical(n_plain, 1), body, 0)

    @pl.when((n_plain & 1) == 1)
    def _():
        pair_step(n_plain - 1, False)

    pair_step(qi - 1, True)
    diagonal_tail()


def _diff_attn(qt, k, vt, lam_vecs, subln, lam_init):
    assert ATTN_Q_TILE == TOKEN_TILE and ATTN_Q_TILE == 2 * ATTN_K_TILE
    batch, seq, _ = k.shape
    nqb = seq // ATTN_Q_TILE
    n_chains = 2 * ATTN_HEADS_PER_STEP
    width = ATTN_HEADS_PER_STEP * DIFF_PAIR_W
    stat = pltpu.VMEM((n_chains, 1, ATTN_Q_TILE), F32)
    scores = pltpu.VMEM((n_chains, ATTN_K_TILE, ATTN_Q_TILE), F32)
    return pl.pallas_call(
        functools.partial(_attn_body, lam_init=lam_init),
        grid=(batch, DIFF_HEADS // ATTN_HEADS_PER_STEP),
        in_specs=[
            pl.BlockSpec((1, nqb, width, ATTN_Q_TILE), lambda b, h: (b, 0, h, 0)),
            pl.BlockSpec((1, seq, width), lambda b, h: (b, 0, h)),
            pl.BlockSpec((1, nqb, width, ATTN_Q_TILE), lambda b, h: (b, 0, h, 0)),
            pl.BlockSpec(lam_vecs.shape, lambda b, h: (0, 0)),
            pl.BlockSpec(subln.shape, lambda b, h: (0, 0)),
        ],
        out_specs=pl.BlockSpec((1, nqb, width, ATTN_Q_TILE), lambda b, h: (b, 0, h, 0)),
        out_shape=jax.ShapeDtypeStruct((batch, nqb, D_MODEL, ATTN_Q_TILE), BF16),
        scratch_shapes=[scores, scores, stat, stat, stat,
                        pltpu.VMEM((n_chains, DIFF_PAIR_W + ATTN_SUM_ROWS, ATTN_Q_TILE), F32)],
        compiler_params=_params("parallel", "parallel"),
        name="diff_attn",
    )(qt, k, vt, lam_vecs, subln)


def kernel(x, p, norm_g, w_ffn_in, w_ffn_out, w_ple_proj, w_ple_gate, mlstm_w_in, mlstm_b_gates,
           mlstm_head_norm, mlstm_w_out, kv_norm, w_kv, diff_w_q, diff_lambda, diff_subln, diff_w_out):
    batch, seq, _ = x.shape
    depth = norm_g.shape[0]
    n_a = depth // 2
    t = batch * seq
    assert t % TOKEN_TILE == 0 and seq % MLSTM_SEQ_TILE == 0 and seq % ATTN_Q_TILE == 0
    assert TOKEN_TILE % MLSTM_CHUNK == 0 and MLSTM_SEQ_TILE % TOKEN_TILE == 0

    bf = lambda w: w.astype(BF16)
    x = x.reshape(t, D_MODEL)
    p = p.reshape(depth, t, PLE_DIM)
    w_ffn_in, w_ffn_out, w_ple_gate, w_ple_proj = bf(w_ffn_in), bf(w_ffn_out), bf(w_ple_gate), bf(w_ple_proj)
    k_sh = vt_sh = None
    for layer in range(depth):
        g = (norm_g, (layer,))
        ffn1 = ((w_ffn_in, (layer, 0)), (w_ffn_out, (layer, 0)))
        ffn2 = ((w_ffn_in, (layer, 1)), (w_ffn_out, (layer, 1)))
        ple = ((w_ple_gate, (layer,)), (w_ple_proj, (layer,)))
        if layer < n_a:
            w_in = mlstm_w_in[layer]
            q_end, k_end = MLSTM_QK_W, 2 * MLSTM_QK_W
            v_end, o_end = k_end + D_MODEL, k_end + 2 * D_MODEL
            pad = GATE_ROWS - 2 * MLSTM_HEADS
            wgt = jnp.pad(w_in[:, o_end:].T, ((0, pad), (0, 0)))
            bg = jnp.pad(mlstm_b_gates[layer], (0, pad)).reshape(GATE_ROWS, 1)
            proj = (bf(w_in[:, :q_end]), bf(w_in[:, q_end:k_end].T), bf(w_in[:, k_end:v_end]),
                    bf(w_in[:, v_end:o_end]), bf(wgt), bg)
            x, q, kt, v, o, gates = _pre_mixer(x, g, *ffn1, proj, "mlstm", batch)
            h = _mlstm_core(q, kt, v, o, gates, mlstm_head_norm[layer], batch, seq)
            w_mix = bf(mlstm_w_out[layer])
        else:
            j = layer - n_a
            lam_init = 0.8 - 0.6 * math.exp(-0.3 * layer)
            x, qt = _pre_mixer(x, g, *ffn1, (bf(diff_w_q[j].T),), "attn", batch)
            h = _diff_attn(qt, k_sh.reshape(batch, seq, D_MODEL), vt_sh,
                           diff_lambda[j], diff_subln[j].reshape(DIFF_PAIR_W, 1), lam_init)
            w_mix = bf(diff_w_out[j])
        if layer == n_a - 1:
            x, k_sh, vt_sh = _post_mixer(x, h, (p, layer), g, w_mix, *ffn2, *ple,
                                         kv_norm.reshape(1, D_MODEL), bf(w_kv[:, :D_MODEL]),
                                         bf(w_kv[:, D_MODEL:].T), batch)
        else:
            x = _post_mixer(x, h, (p, layer), g, w_mix, *ffn2, *ple)
    return x.reshape(batch, seq, D_MODEL)
```

```python
import functools
import math

import jax
import jax.numpy as jnp
from jax import lax
from jax.experimental import pallas as pl
from jax.experimental.pallas import tpu as pltpu

F32 = jnp.float32
BF16 = jnp.bfloat16

D_MODEL = 1024
D_FF = 2816
PLE_DIM = 256
EPS = 1e-6
LOG2E = math.log2(math.e)

MLSTM_HEADS = 4
MLSTM_DV = 256
MLSTM_DK = 128
MLSTM_CHUNK = 128
MLSTM_QK_W = MLSTM_HEADS * MLSTM_DK
GATE_ROWS = 16

DIFF_HEAD_DIM = 64
DIFF_HEADS = 8
DIFF_PAIR_W = 2 * DIFF_HEAD_DIM

VMEM_LIMIT_BYTES = 56 * 1024 * 1024

TOKEN_TILE = 512
FFN_CHUNK = 256
MLSTM_SEQ_TILE = 2048
ATTN_Q_TILE = 512
ATTN_K_TILE = 256
ATTN_SUM_ROWS = 16
ATTN_HEADS_PER_STEP = 2


def _rms(x, g):
    ms = jnp.mean(x * x, axis=-1, keepdims=True)
    return (x * lax.rsqrt(ms + EPS)) * g


def _dot(a, b):
    return jnp.dot(a, b, preferred_element_type=F32)


def _dot_nt(a, b):
    return lax.dot_general(a, b, (((1,), (1,)), ((), ())), preferred_element_type=F32)


def _params(*semantics):
    return pltpu.CompilerParams(dimension_semantics=semantics, vmem_limit_bytes=VMEM_LIMIT_BYTES)


def _resident(w):
    arr, lead = w if isinstance(w, tuple) else (w, ())
    tail = arr.shape[len(lead):]
    index = tuple(lead) + (0,) * len(tail)
    return pl.BlockSpec((None,) * len(lead) + tail, lambda *_: index, pipeline_mode=pl.Buffered(1))


def _operand(w):
    return w[0] if isinstance(w, tuple) else w


def _row_tile(width, tile=TOKEN_TILE):
    return pl.BlockSpec((tile, width), lambda i: (i, 0))


def _feature_major_tile(tiles_per_seq):
    return pl.BlockSpec((1, 1, D_MODEL, TOKEN_TILE), lambda i: (i // tiles_per_seq, i % tiles_per_seq, 0, 0))


def _ffn_math(x, g_ref, win_ref, wout_ref, pre, post):
    xn = _rms(x, g_ref[pre:pre + 1, :]).astype(BF16)
    acc = jnp.zeros(x.shape, F32)
    for c in range(D_FF // FFN_CHUNK):
        lo = c * FFN_CHUNK
        gate = _dot(xn, win_ref[:, lo:lo + FFN_CHUNK])
        up = _dot(xn, win_ref[:, D_FF + lo:D_FF + lo + FFN_CHUNK])
        act = (gate * jax.nn.sigmoid(gate) * up).astype(BF16)
        acc = acc + _dot(act, wout_ref[lo:lo + FFN_CHUNK, :])
    return x + 0.5 * _rms(acc, g_ref[post:post + 1, :])


def _mlstm_proj_math(x, g_ref, wq_ref, wkgt_ref, wv_ref, wo_ref, bg_ref,
                     q_ref, kt_ref, v_ref, o_ref, gates_ref):
    xn = _rms(x, g_ref[2:3, :]).astype(BF16)
    q_ref[...] = _dot(xn, wq_ref[...]).astype(BF16)
    v_ref[...] = _dot(xn, wv_ref[...]).astype(BF16)
    o_ref[...] = _dot(xn, wo_ref[...])
    kg = _dot_nt(wkgt_ref[...], xn)
    kt = (kg[:MLSTM_QK_W] * (MLSTM_DK ** -0.5)).astype(BF16)
    z = kg[MLSTM_QK_W:] + bg_ref[...]
    row = lax.broadcasted_iota(jnp.int32, z.shape, 0)
    is_forget = (row >= MLSTM_HEADS) & (row < 2 * MLSTM_HEADS)
    gates = jnp.where(is_forget, jax.nn.log_sigmoid(z), z)
    for c in range(TOKEN_TILE // MLSTM_CHUNK):
        sl = slice(c * MLSTM_CHUNK, (c + 1) * MLSTM_CHUNK)
        kt_ref[c] = kt[:, sl]
        gates_ref[c] = gates[:, sl]


def _q_proj_math(x, g_ref, wt_ref, qt_ref):
    xn = _rms(x, g_ref[2:3, :]).astype(BF16)
    qt_ref[0, 0] = (_dot_nt(wt_ref[...], xn) * (DIFF_HEAD_DIM ** -0.5 * LOG2E)).astype(BF16)


def _pre_mixer_body(x_ref, g_ref, win_ref, wout_ref, *rest, mixer):
    *w_refs, y_ref = rest[:-5] if mixer == "mlstm" else rest[:-1]
    out_refs = rest[-5:] if mixer == "mlstm" else rest[-1:]
    y = _ffn_math(x_ref[...], g_ref, win_ref, wout_ref, 0, 1)
    y_ref[...] = y
    if mixer == "mlstm":
        _mlstm_proj_math(y, g_ref, *w_refs, *out_refs)
    else:
        _q_proj_math(y, g_ref, *w_refs, *out_refs)


def _pre_mixer(x, g, w_in, w_out, proj_weights, mixer, batch):
    t = x.shape[0]
    nsb = t // batch // TOKEN_TILE
    if mixer == "mlstm":
        cpt = TOKEN_TILE // MLSTM_CHUNK
        nchunks = t // MLSTM_CHUNK
        proj_specs = [
            _row_tile(MLSTM_QK_W),
            pl.BlockSpec((cpt, MLSTM_QK_W, MLSTM_CHUNK), lambda i: (i, 0, 0)),
            _row_tile(D_MODEL),
            _row_tile(D_MODEL),
            pl.BlockSpec((cpt, GATE_ROWS, MLSTM_CHUNK), lambda i: (i, 0, 0)),
        ]
        proj_shapes = [
            jax.ShapeDtypeStruct((t, MLSTM_QK_W), BF16),
            jax.ShapeDtypeStruct((nchunks, MLSTM_QK_W, MLSTM_CHUNK), BF16),
            jax.ShapeDtypeStruct((t, D_MODEL), BF16),
            jax.ShapeDtypeStruct((t, D_MODEL), F32),
            jax.ShapeDtypeStruct((nchunks, GATE_ROWS, MLSTM_CHUNK), F32),
        ]
    else:
        proj_specs = [_feature_major_tile(nsb)]
        proj_shapes = [jax.ShapeDtypeStruct((batch, nsb, D_MODEL, TOKEN_TILE), BF16)]
    return pl.pallas_call(
        functools.partial(_pre_mixer_body, mixer=mixer),
        grid=(t // TOKEN_TILE,),
        in_specs=[_row_tile(D_MODEL)] + [_resident(w) for w in (g, w_in, w_out, *proj_weights)],
        out_specs=[_row_tile(D_MODEL)] + proj_specs,
        out_shape=[jax.ShapeDtypeStruct(x.shape, F32)] + proj_shapes,
        compiler_params=_params("parallel"),
        name="pre_" + mixer,
    )(x, *[_operand(w) for w in (g, w_in, w_out, *proj_weights)])


def _post_mixer_body(x_ref, h_ref, p_ref, g_ref, wmix_ref, win_ref, wout_ref, wgate_ref, wple_ref,
                     *rest, with_kv, h_feature_major):
    if with_kv:
        kvn_ref, wk_ref, wvt_ref, o_ref, k_ref, vt_ref = rest
    else:
        (o_ref,) = rest
    x = x_ref[...]
    if h_feature_major:
        mixed = lax.dot_general(h_ref[0, 0], wmix_ref[...], (((0,), (0,)), ((), ())),
                                preferred_element_type=F32)
    else:
        mixed = _dot(h_ref[...], wmix_ref[...])
    x = x + _rms(mixed, g_ref[3:4, :])
    x = _ffn_math(x, g_ref, win_ref, wout_ref, 4, 5)
    xn = _rms(x, g_ref[6:7, :]).astype(BF16)
    gate = jax.nn.sigmoid(_dot(xn, wgate_ref[...]))
    e = _dot(p_ref[...].astype(BF16), wple_ref[...])
    y = x + _rms(e * gate, g_ref[7:8, :])
    o_ref[...] = y
    if with_kv:
        yn = _rms(y, kvn_ref[...]).astype(BF16)
        k_ref[...] = _dot(yn, wk_ref[...]).astype(BF16)
        vt_ref[0, 0] = _dot_nt(wvt_ref[...], yn).astype(BF16)


def _post_mixer(x, h, p, g, w_mix, w_in, w_out, w_gate, w_ple, kv_norm=None, w_k=None, w_vt=None, batch=None):
    t = x.shape[0]
    with_kv = w_k is not None
    p_all, layer = p
    p_spec = pl.BlockSpec((None, TOKEN_TILE, PLE_DIM), lambda i: (layer, i, 0))
    h_feature_major = h.ndim == 4
    h_spec = _feature_major_tile(h.shape[1]) if h_feature_major else _row_tile(D_MODEL)
    weights = [g, w_mix, w_in, w_out, w_gate, w_ple]
    out_specs = _row_tile(D_MODEL)
    out_shape = jax.ShapeDtypeStruct(x.shape, F32)
    if with_kv:
        nsb = t // batch // TOKEN_TILE
        weights += [kv_norm, w_k, w_vt]
        out_specs = [out_specs, _row_tile(D_MODEL), _feature_major_tile(nsb)]
        out_shape = [out_shape, jax.ShapeDtypeStruct((t, D_MODEL), BF16),
                     jax.ShapeDtypeStruct((batch, nsb, D_MODEL, TOKEN_TILE), BF16)]
    return pl.pallas_call(
        functools.partial(_post_mixer_body, with_kv=with_kv, h_feature_major=h_feature_major),
        grid=(t // TOKEN_TILE,),
        in_specs=[_row_tile(D_MODEL), h_spec, p_spec] + [_resident(w) for w in weights],
        out_specs=out_specs,
        out_shape=out_shape,
        compiler_params=_params("parallel"),
        name="post_mixer_kv" if with_kv else "post_mixer",
    )(x, h, p_all, *[_operand(w) for w in weights])


def _split3(x):
    hi = x.astype(BF16)
    r = x - hi.astype(F32)
    mid = r.astype(BF16)
    lo = (r - mid.astype(F32)).astype(BF16)
    return hi, mid, lo


def _mlstm_core_body(q_ref, kt_ref, v_ref, o_ref, gates_ref, hn_ref, out_ref,
                     c_ref, m_ref, ext_ref, u_ref, den_ref, mloc_ref, bcol_ref, blast_ref, mg_ref):
    L = MLSTM_CHUNK
    n_chunks = MLSTM_SEQ_TILE // L
    dk, dv = MLSTM_DK, MLSTM_DV

    @pl.when(pl.program_id(1) == 0)
    def _():
        c_ref[...] = jnp.zeros(c_ref.shape, F32)
        m_ref[...] = jnp.full(m_ref.shape, -jnp.inf, F32)

    r_i = lax.broadcasted_iota(jnp.int32, (L, L), 0)
    c_i = lax.broadcasted_iota(jnp.int32, (L, L), 1)
    causal = c_i <= r_i
    upper = (r_i <= c_i).astype(BF16)
    ones_blk = jnp.ones((L, 128), BF16)

    def prepare(c, slot):
        row0 = pl.multiple_of(c * L, L)
        g_all = gates_ref[c]
        b_all = sum(_dot(part, upper) for part in _split3(g_all))
        for h in range(MLSTM_HEADS):
            li = g_all[h:h + 1, :]
            b_row = b_all[MLSTM_HEADS + h:MLSTM_HEADS + h + 1, :]
            q = q_ref[pl.ds(row0, L), h * dk:(h + 1) * dk]
            kt = kt_ref[c, h * dk:(h + 1) * dk, :]
            v = v_ref[pl.ds(row0, L), h * dv:(h + 1) * dv]
            v_ext = jnp.concatenate([v, ones_blk], axis=1)

            b_col = jnp.broadcast_to(b_row, (L, L)).T
            a_row = li - b_row
            d = jnp.where(causal, b_col + a_row, -jnp.inf)
            m_loc = jnp.max(d, axis=-1, keepdims=True)
            qk = _dot(q, kt) * jnp.exp(d - m_loc)
            ext_ref[slot, h] = _dot(qk.astype(BF16), v)
            den_ref[slot, h] = jnp.sum(qk, axis=-1, keepdims=True)
            mloc_ref[slot, h] = m_loc
            bcol_ref[slot, h] = b_col[:, 0:1]

            b_last = b_row[:, L - 1:L]
            g_row = b_last + a_row
            mg = jnp.max(g_row, axis=-1, keepdims=True)
            kts = (kt.astype(F32) * jnp.exp(g_row - mg)).astype(BF16)
            u_ref[slot, h] = _dot(kts, v_ext)
            blast_ref[slot, h] = jnp.broadcast_to(b_last, (1, 128))
            mg_ref[slot, h] = jnp.broadcast_to(mg, (1, 128))

    def advance(c, slot):
        row0 = pl.multiple_of(c * L, L)
        for h in range(MLSTM_HEADS):
            m_prev = m_ref[h:h + 1, 0:1]
            q = q_ref[pl.ds(row0, L), h * dk:(h + 1) * dk]
            m_loc = mloc_ref[slot, h]
            inter = bcol_ref[slot, h] + m_prev
            m_t = jnp.maximum(inter, m_loc)
            r_loc = jnp.exp(m_loc - m_t)
            s_inter = jnp.exp(inter - m_t)
            carried = _dot(q, c_ref[h].astype(BF16))
            num = ext_ref[slot, h] * r_loc + s_inter * carried[:, :dv]
            den = den_ref[slot, h] * r_loc + s_inter * carried[:, dv:dv + 1]
            hid = num / jnp.maximum(jnp.abs(den), jnp.exp(-m_t))

            hn = _rms(hid, hn_ref[h:h + 1, :])
            og = o_ref[pl.ds(row0, L), h * dv:(h + 1) * dv]
            out_ref[pl.ds(row0, L), h * dv:(h + 1) * dv] = (jax.nn.sigmoid(og) * hn).astype(BF16)

            b_last = blast_ref[slot, h][:, 0:1]
            mg = mg_ref[slot, h][:, 0:1]
            m_new = jnp.maximum(b_last + m_prev, mg)
            c_ref[h] = jnp.exp(b_last + m_prev - m_new) * c_ref[h] + jnp.exp(mg - m_new) * u_ref[slot, h]
            m_ref[h:h + 1, :] = jnp.broadcast_to(m_new, (1, 128))

    prepare(0, 0)

    def two_chunks(i, carry):
        c = 2 * i
        prepare(c + 1, 1)
        advance(c, 0)
        prepare(c + 2, 0)
        advance(c + 1, 1)
        return carry

    lax.fori_loop(0, n_chunks // 2 - 1, two_chunks, 0)
    prepare(n_chunks - 1, 1)
    advance(n_chunks - 2, 0)
    advance(n_chunks - 1, 1)


def _mlstm_core(q, kt, v, o, gates, head_norm, batch, seq):
    t = q.shape[0]
    ns = seq // MLSTM_SEQ_TILE
    cps = MLSTM_SEQ_TILE // MLSTM_CHUNK

    def rows(width):
        return pl.BlockSpec((MLSTM_SEQ_TILE, width), lambda b, s: (b * ns + s, 0))

    def slabs(height):
        return pl.BlockSpec((cps, height, MLSTM_CHUNK), lambda b, s: (b * ns + s, 0, 0))

    assert cps % 2 == 0 and cps >= 4
    col_stat = pltpu.VMEM((2, MLSTM_HEADS, MLSTM_CHUNK, 1), F32)
    scalar_stat = pltpu.VMEM((2, MLSTM_HEADS, 1, 128), F32)

    return pl.pallas_call(
        _mlstm_core_body,
        grid=(batch, ns),
        in_specs=[rows(MLSTM_QK_W), slabs(MLSTM_QK_W), rows(D_MODEL), rows(D_MODEL), slabs(GATE_ROWS),
                  pl.BlockSpec(head_norm.shape, lambda b, s: (0, 0))],
        out_specs=rows(D_MODEL),
        out_shape=jax.ShapeDtypeStruct((t, D_MODEL), BF16),
        scratch_shapes=[pltpu.VMEM((MLSTM_HEADS, MLSTM_DK, MLSTM_DV + 128), F32),
                        pltpu.VMEM((8, 128), F32),
                        pltpu.VMEM((2, MLSTM_HEADS, MLSTM_CHUNK, MLSTM_DV), F32),
                        pltpu.VMEM((2, MLSTM_HEADS, MLSTM_DK, MLSTM_DV + 128), F32),
                        col_stat, col_stat, col_stat, scalar_stat, scalar_stat],
        compiler_params=_params("parallel", "arbitrary"),
        name="mlstm_core",
    )(q, kt, v, o, gates, head_norm)


def _attn_body(qt_ref, k_ref, vt_ref, lam_ref, subln_ref, o_ref,
               sa_ref, sb_ref, bma_ref, bmb_ref, m_ref, acc_ref, *, lam_init):
    refs = (qt_ref, k_ref, vt_ref, lam_ref, subln_ref, o_ref, sa_ref, sb_ref, bma_ref, bmb_ref, m_ref, acc_ref)
    n_tiles = qt_ref.shape[1]
    _attn_query_tile(jnp.int32(0), refs, lam_init, first=True)

    def later_tile(qi, carry):
        _attn_query_tile(qi, refs, lam_init, first=False)
        return carry

    lax.fori_loop(1, n_tiles, later_tile, 0)
    _attn_write_tile(n_tiles - 1, refs, lam_init)


def _attn_write_tile(qi, refs, lam_init):
    _, _, _, lam_ref, subln_ref, o_ref, _, _, _, _, _, acc_ref = refs
    dv = DIFF_PAIR_W
    lv = lam_ref[...]
    lam = (jnp.exp(jnp.sum(lv[0:1, :] * lv[1:2, :], axis=-1, keepdims=True))
           - jnp.exp(jnp.sum(lv[2:3, :] * lv[3:4, :], axis=-1, keepdims=True)) + lam_init)
    for head in range(ATTN_HEADS_PER_STEP):
        c1, c2 = 2 * head, 2 * head + 1
        ot = (acc_ref[c1, :dv, :] / acc_ref[c1, dv:dv + 1, :]
              - lam * (acc_ref[c2, :dv, :] / acc_ref[c2, dv:dv + 1, :]))
        ms = jnp.mean(ot * ot, axis=0, keepdims=True)
        o_ref[0, qi, head * dv:(head + 1) * dv, :] = (
            ((ot * lax.rsqrt(ms + EPS)) * subln_ref[...]) * (1.0 - lam_init)).astype(BF16)


def _attn_query_tile(qi, refs, lam_init, *, first):
    qt_ref, k_ref, vt_ref, lam_ref, subln_ref, o_ref, sa_ref, sb_ref, bma_ref, bmb_ref, m_ref, acc_ref = refs
    tq, tk = ATTN_Q_TILE, ATTN_K_TILE
    dv = DIFF_PAIR_W
    n_chains = 2 * ATTN_HEADS_PER_STEP
    if not first:
        _attn_write_tile(qi - 1, refs, lam_init)
    qt = qt_ref[0, qi]
    row = lax.broadcasted_iota(jnp.int32, qt.shape, 0)
    zero = jnp.zeros_like(qt)
    qts = [jnp.where((row >= c * DIFF_HEAD_DIM) & (row < (c + 1) * DIFF_HEAD_DIM), qt, zero)
           for c in range(n_chains)]
    acc_ref[...] = jnp.zeros(acc_ref.shape, F32)
    m_ref[...] = jnp.full(m_ref.shape, -jnp.inf, F32)

    def score(pair, half, s_ref, bm_ref, mask_offset, q0=0):
        k0 = pl.multiple_of(pair * tq + half * tk, tk)
        k = k_ref[0, pl.ds(k0, tk), :]
        for c, qm in enumerate(qts):
            s = _dot(k, qm[:, q0:])
            if mask_offset is not None:
                key = lax.broadcasted_iota(jnp.int32, s.shape, 0) + mask_offset
                qry = lax.broadcasted_iota(jnp.int32, s.shape, 1) + q0
                s = jnp.where(key <= qry, s, -jnp.inf)
            s_ref[c, :, q0:] = s
            bm_ref[c, :, q0:] = jnp.max(s, axis=0, keepdims=True)

    ones_rows = jnp.ones((ATTN_SUM_ROWS, tk), BF16)

    def absorb(pair, half, s_ref, bm_ref, q0=0):
        for c in range(n_chains):
            head = c // 2
            vt = vt_ref[0, pair, head * dv:(head + 1) * dv, half * tk:(half + 1) * tk]
            vt_ext = jnp.concatenate([vt, ones_rows], axis=0)
            m = m_ref[c, :, q0:]
            m_new = jnp.maximum(m, bm_ref[c, :, q0:])
            alpha = jnp.exp2(m - m_new)
            p = jnp.exp2(s_ref[c, :, q0:] - m_new)
            acc_ref[c, :, q0:] = alpha * acc_ref[c, :, q0:] + _dot(vt_ext, p.astype(BF16))
            m_ref[c, :, q0:] = m_new

    def pair_step(pair, next_is_diagonal):
        score(pair, 1, sb_ref, bmb_ref, None)
        absorb(pair, 0, sa_ref, bma_ref)
        score(pair + 1, 0, sa_ref, bma_ref, 0 if next_is_diagonal else None)
        absorb(pair, 1, sb_ref, bmb_ref)

    def diagonal_tail():
        score(qi, 1, sb_ref, bmb_ref, tk, q0=tk)
        absorb(qi, 0, sa_ref, bma_ref)
        absorb(qi, 1, sb_ref, bmb_ref, q0=tk)

    if first:
        score(0, 0, sa_ref, bma_ref, 0)
        diagonal_tail()
        return

    score(0, 0, sa_ref, bma_ref, None)
    n_plain = qi - 1

    def body(i, carry):
        pair_step(2 * i, False)
        pair_step(2 * i + 1, False)
        return carry

    lax.fori_loop(0, lax.shift_right_logical(n_plain, 1), body, 0)

    @pl.when((n_plain & 1) == 1)
    def _():
        pair_step(n_plain - 1, False)

    pair_step(qi - 1, True)
    diagonal_tail()


def _diff_attn(qt, k, vt, lam_vecs, subln, lam_init):
    assert ATTN_Q_TILE == TOKEN_TILE and ATTN_Q_TILE == 2 * ATTN_K_TILE
    batch, seq, _ = k.shape
    nqb = seq // ATTN_Q_TILE
    n_chains = 2 * ATTN_HEADS_PER_STEP
    width = ATTN_HEADS_PER_STEP * DIFF_PAIR_W
    stat = pltpu.VMEM((n_chains, 1, ATTN_Q_TILE), F32)
    scores = pltpu.VMEM((n_chains, ATTN_K_TILE, ATTN_Q_TILE), F32)
    return pl.pallas_call(
        functools.partial(_attn_body, lam_init=lam_init),
        grid=(batch, DIFF_HEADS // ATTN_HEADS_PER_STEP),
        in_specs=[
            pl.BlockSpec((1, nqb, width, ATTN_Q_TILE), lambda b, h: (b, 0, h, 0)),
            pl.BlockSpec((1, seq, width), lambda b, h: (b, 0, h)),
            pl.BlockSpec((1, nqb, width, ATTN_Q_TILE), lambda b, h: (b, 0, h, 0)),
            pl.BlockSpec(lam_vecs.shape, lambda b, h: (0, 0)),
            pl.BlockSpec(subln.shape, lambda b, h: (0, 0)),
        ],
        out_specs=pl.BlockSpec((1, nqb, width, ATTN_Q_TILE), lambda b, h: (b, 0, h, 0)),
        out_shape=jax.ShapeDtypeStruct((batch, nqb, D_MODEL, ATTN_Q_TILE), BF16),
        scratch_shapes=[scores, scores, stat, stat, stat,
                        pltpu.VMEM((n_chains, DIFF_PAIR_W + ATTN_SUM_ROWS, ATTN_Q_TILE), F32)],
        compiler_params=_params("parallel", "parallel"),
        name="diff_attn",
    )(qt, k, vt, lam_vecs, subln)


def kernel(x, p, norm_g, w_ffn_in, w_ffn_out, w_ple_proj, w_ple_gate, mlstm_w_in, mlstm_b_gates,
           mlstm_head_norm, mlstm_w_out, kv_norm, w_kv, diff_w_q, diff_lambda, diff_subln, diff_w_out):
    batch, seq, _ = x.shape
    depth = norm_g.shape[0]
    n_a = depth // 2
    t = batch * seq
    assert t % TOKEN_TILE == 0 and seq % MLSTM_SEQ_TILE == 0 and seq % ATTN_Q_TILE == 0
    assert TOKEN_TILE % MLSTM_CHUNK == 0 and MLSTM_SEQ_TILE % TOKEN_TILE == 0

    bf = lambda w: w.astype(BF16)
    x = x.reshape(t, D_MODEL)
    p = p.reshape(depth, t, PLE_DIM)
    w_ffn_in, w_ffn_out, w_ple_gate, w_ple_proj = bf(w_ffn_in), bf(w_ffn_out), bf(w_ple_gate), bf(w_ple_proj)
    k_sh = vt_sh = None
    for layer in range(depth):
        g = (norm_g, (layer,))
        ffn1 = ((w_ffn_in, (layer, 0)), (w_ffn_out, (layer, 0)))
        ffn2 = ((w_ffn_in, (layer, 1)), (w_ffn_out, (layer, 1)))
        ple = ((w_ple_gate, (layer,)), (w_ple_proj, (layer,)))
        if layer < n_a:
            w_in = mlstm_w_in[layer]
            q_end, k_end = MLSTM_QK_W, 2 * MLSTM_QK_W
            v_end, o_end = k_end + D_MODEL, k_end + 2 * D_MODEL
            pad = GATE_ROWS - 2 * MLSTM_HEADS
            wgt = jnp.pad(w_in[:, o_end:].T, ((0, pad), (0, 0)))
            bg = jnp.pad(mlstm_b_gates[layer], (0, pad)).reshape(GATE_ROWS, 1)
            wkgt = jnp.concatenate([w_in[:, q_end:k_end].T, wgt], axis=0)
            proj = (bf(w_in[:, :q_end]), bf(wkgt), bf(w_in[:, k_end:v_end]), bf(w_in[:, v_end:o_end]), bg)
            x, q, kt, v, o, gates = _pre_mixer(x, g, *ffn1, proj, "mlstm", batch)
            h = _mlstm_core(q, kt, v, o, gates, mlstm_head_norm[layer], batch, seq)
            w_mix = bf(mlstm_w_out[layer])
        else:
            j = layer - n_a
            lam_init = 0.8 - 0.6 * math.exp(-0.3 * layer)
            x, qt = _pre_mixer(x, g, *ffn1, (bf(diff_w_q[j].T),), "attn", batch)
            h = _diff_attn(qt, k_sh.reshape(batch, seq, D_MODEL), vt_sh,
                           diff_lambda[j], diff_subln[j].reshape(DIFF_PAIR_W, 1), lam_init)
            w_mix = bf(diff_w_out[j])
        if layer == n_a - 1:
            x, k_sh, vt_sh = _post_mixer(x, h, (p, layer), g, w_mix, *ffn2, *ple,
                                         kv_norm.reshape(1, D_MODEL), bf(w_kv[:, :D_MODEL]),
                                         bf(w_kv[:, D_MODEL:].T), batch)
        else:
            x = _post_mixer(x, h, (p, layer), g, w_mix, *ffn2, *ple)
    return x.reshape(batch, seq, D_MODEL)
```

```python
import functools
import math

import jax
import jax.numpy as jnp
from jax import lax
from jax.experimental import pallas as pl
from jax.experimental.pallas import tpu as pltpu

F32 = jnp.float32
BF16 = jnp.bfloat16

D_MODEL = 1024
D_FF = 2816
PLE_DIM = 256
EPS = 1e-6
LOG2E = math.log2(math.e)

MLSTM_HEADS = 4
MLSTM_DV = 256
MLSTM_DK = 128
MLSTM_CHUNK = 128
MLSTM_QK_W = MLSTM_HEADS * MLSTM_DK
GATE_ROWS = 16

DIFF_HEAD_DIM = 64
DIFF_HEADS = 8
DIFF_PAIR_W = 2 * DIFF_HEAD_DIM

VMEM_LIMIT_BYTES = 56 * 1024 * 1024

TOKEN_TILE = 512
FFN_CHUNK = 256
MLSTM_SEQ_TILE = 2048
ATTN_Q_TILE = 512
ATTN_K_TILE = 256
ATTN_SUM_ROWS = 16
ATTN_HEADS_PER_STEP = 2


def _rms(x, g):
    ms = jnp.mean(x * x, axis=-1, keepdims=True)
    return (x * lax.rsqrt(ms + EPS)) * g


def _dot(a, b):
    return jnp.dot(a, b, preferred_element_type=F32)


def _dot_nt(a, b):
    return lax.dot_general(a, b, (((1,), (1,)), ((), ())), preferred_element_type=F32)


def _params(*semantics):
    return pltpu.CompilerParams(dimension_semantics=semantics, vmem_limit_bytes=VMEM_LIMIT_BYTES)


def _resident(w):
    arr, lead = w if isinstance(w, tuple) else (w, ())
    tail = arr.shape[len(lead):]
    index = tuple(lead) + (0,) * len(tail)
    return pl.BlockSpec((None,) * len(lead) + tail, lambda *_: index, pipeline_mode=pl.Buffered(1))


def _operand(w):
    return w[0] if isinstance(w, tuple) else w


def _row_tile(width, tile=TOKEN_TILE):
    return pl.BlockSpec((tile, width), lambda i: (i, 0))


def _feature_major_tile(tiles_per_seq):
    return pl.BlockSpec((1, 1, D_MODEL, TOKEN_TILE), lambda i: (i // tiles_per_seq, i % tiles_per_seq, 0, 0))


def _ffn_math(x, g_ref, win_ref, wout_ref, pre, post):
    xn = _rms(x, g_ref[pre:pre + 1, :]).astype(BF16)
    acc = jnp.zeros(x.shape, F32)
    for c in range(D_FF // FFN_CHUNK):
        lo = c * FFN_CHUNK
        gate = _dot(xn, win_ref[:, lo:lo + FFN_CHUNK])
        up = _dot(xn, win_ref[:, D_FF + lo:D_FF + lo + FFN_CHUNK])
        act = (gate * jax.nn.sigmoid(gate) * up).astype(BF16)
        acc = acc + _dot(act, wout_ref[lo:lo + FFN_CHUNK, :])
    return x + 0.5 * _rms(acc, g_ref[post:post + 1, :])


def _mlstm_proj_math(x, g_ref, wq_ref, wkgt_ref, wv_ref, wo_ref, bg_ref,
                     q_ref, kt_ref, v_ref, o_ref, gates_ref):
    xn = _rms(x, g_ref[2:3, :]).astype(BF16)
    q_ref[...] = _dot(xn, wq_ref[...]).astype(BF16)
    v_ref[...] = _dot(xn, wv_ref[...]).astype(BF16)
    o_ref[...] = _dot(xn, wo_ref[...])
    kg = _dot_nt(wkgt_ref[...], xn)
    kt = (kg[:MLSTM_QK_W] * (MLSTM_DK ** -0.5)).astype(BF16)
    z = kg[MLSTM_QK_W:] + bg_ref[...]
    row = lax.broadcasted_iota(jnp.int32, z.shape, 0)
    is_forget = (row >= MLSTM_HEADS) & (row < 2 * MLSTM_HEADS)
    gates = jnp.where(is_forget, jax.nn.log_sigmoid(z), z)
    for c in range(TOKEN_TILE // MLSTM_CHUNK):
        sl = slice(c * MLSTM_CHUNK, (c + 1) * MLSTM_CHUNK)
        kt_ref[c] = kt[:, sl]
        gates_ref[c] = gates[:, sl]


def _q_proj_math(x, g_ref, wt_ref, qt_ref):
    xn = _rms(x, g_ref[2:3, :]).astype(BF16)
    qt_ref[0, 0] = (_dot_nt(wt_ref[...], xn) * (DIFF_HEAD_DIM ** -0.5 * LOG2E)).astype(BF16)


def _pre_mixer_body(x_ref, g_ref, win_ref, wout_ref, *rest, mixer):
    *w_refs, y_ref = rest[:-5] if mixer == "mlstm" else rest[:-1]
    out_refs = rest[-5:] if mixer == "mlstm" else rest[-1:]
    y = _ffn_math(x_ref[...], g_ref, win_ref, wout_ref, 0, 1)
    y_ref[...] = y
    if mixer == "mlstm":
        _mlstm_proj_math(y, g_ref, *w_refs, *out_refs)
    else:
        _q_proj_math(y, g_ref, *w_refs, *out_refs)


def _pre_mixer(x, g, w_in, w_out, proj_weights, mixer, batch):
    t = x.shape[0]
    nsb = t // batch // TOKEN_TILE
    if mixer == "mlstm":
        cpt = TOKEN_TILE // MLSTM_CHUNK
        nchunks = t // MLSTM_CHUNK
        proj_specs = [
            _row_tile(MLSTM_QK_W),
            pl.BlockSpec((cpt, MLSTM_QK_W, MLSTM_CHUNK), lambda i: (i, 0, 0)),
            _row_tile(D_MODEL),
            _row_tile(D_MODEL),
            pl.BlockSpec((cpt, GATE_ROWS, MLSTM_CHUNK), lambda i: (i, 0, 0)),
        ]
        proj_shapes = [
            jax.ShapeDtypeStruct((t, MLSTM_QK_W), BF16),
            jax.ShapeDtypeStruct((nchunks, MLSTM_QK_W, MLSTM_CHUNK), BF16),
            jax.ShapeDtypeStruct((t, D_MODEL), BF16),
            jax.ShapeDtypeStruct((t, D_MODEL), F32),
            jax.ShapeDtypeStruct((nchunks, GATE_ROWS, MLSTM_CHUNK), F32),
        ]
    else:
        proj_specs = [_feature_major_tile(nsb)]
        proj_shapes = [jax.ShapeDtypeStruct((batch, nsb, D_MODEL, TOKEN_TILE), BF16)]
    return pl.pallas_call(
        functools.partial(_pre_mixer_body, mixer=mixer),
        grid=(t // TOKEN_TILE,),
        in_specs=[_row_tile(D_MODEL)] + [_resident(w) for w in (g, w_in, w_out, *proj_weights)],
        out_specs=[_row_tile(D_MODEL)] + proj_specs,
        out_shape=[jax.ShapeDtypeStruct(x.shape, F32)] + proj_shapes,
        compiler_params=_params("parallel"),
        name="pre_" + mixer,
    )(x, *[_operand(w) for w in (g, w_in, w_out, *proj_weights)])


def _post_mixer_body(x_ref, h_ref, p_ref, g_ref, wmix_ref, win_ref, wout_ref, wgate_ref, wple_ref,
                     *rest, with_kv, h_feature_major):
    if with_kv:
        kvn_ref, wk_ref, wvt_ref, o_ref, k_ref, vt_ref = rest
    else:
        (o_ref,) = rest
    x = x_ref[...]
    if h_feature_major:
        mixed = lax.dot_general(h_ref[0, 0], wmix_ref[...], (((0,), (0,)), ((), ())),
                                preferred_element_type=F32)
    else:
        mixed = _dot(h_ref[...], wmix_ref[...])
    x = x + _rms(mixed, g_ref[3:4, :])
    x = _ffn_math(x, g_ref, win_ref, wout_ref, 4, 5)
    e = _dot(p_ref[...].astype(BF16), wple_ref[...])
    xn = _rms(x, g_ref[6:7, :]).astype(BF16)
    gate = jax.nn.sigmoid(_dot(xn, wgate_ref[...]))
    y = x + _rms(e * gate, g_ref[7:8, :])
    o_ref[...] = y
    if with_kv:
        yn = _rms(y, kvn_ref[...]).astype(BF16)
        k_ref[...] = _dot(yn, wk_ref[...]).astype(BF16)
        vt_ref[0, 0] = _dot_nt(wvt_ref[...], yn).astype(BF16)


def _post_mixer(x, h, p, g, w_mix, w_in, w_out, w_gate, w_ple, kv_norm=None, w_k=None, w_vt=None, batch=None):
    t = x.shape[0]
    with_kv = w_k is not None
    p_all, layer = p
    p_spec = pl.BlockSpec((None, TOKEN_TILE, PLE_DIM), lambda i: (layer, i, 0))
    h_feature_major = h.ndim == 4
    h_spec = _feature_major_tile(h.shape[1]) if h_feature_major else _row_tile(D_MODEL)
    weights = [g, w_mix, w_in, w_out, w_gate, w_ple]
    out_specs = _row_tile(D_MODEL)
    out_shape = jax.ShapeDtypeStruct(x.shape, F32)
    if with_kv:
        nsb = t // batch // TOKEN_TILE
        weights += [kv_norm, w_k, w_vt]
        out_specs = [out_specs, _row_tile(D_MODEL), _feature_major_tile(nsb)]
        out_shape = [out_shape, jax.ShapeDtypeStruct((t, D_MODEL), BF16),
                     jax.ShapeDtypeStruct((batch, nsb, D_MODEL, TOKEN_TILE), BF16)]
    return pl.pallas_call(
        functools.partial(_post_mixer_body, with_kv=with_kv, h_feature_major=h_feature_major),
        grid=(t // TOKEN_TILE,),
        in_specs=[_row_tile(D_MODEL), h_spec, p_spec] + [_resident(w) for w in weights],
        out_specs=out_specs,
        out_shape=out_shape,
        compiler_params=_params("parallel"),
        name="post_mixer_kv" if with_kv else "post_mixer",
    )(x, h, p_all, *[_operand(w) for w in weights])


def _split3(x):
    hi = x.astype(BF16)
    r = x - hi.astype(F32)
    mid = r.astype(BF16)
    lo = (r - mid.astype(F32)).astype(BF16)
    return hi, mid, lo


def _mlstm_core_body(q_ref, kt_ref, v_ref, o_ref, gates_ref, hn_ref, out_ref,
                     c_ref, m_ref, ext_ref, u_ref, den_ref, mloc_ref, bcol_ref, blast_ref, mg_ref):
    L = MLSTM_CHUNK
    n_chunks = MLSTM_SEQ_TILE // L
    dk, dv = MLSTM_DK, MLSTM_DV

    @pl.when(pl.program_id(1) == 0)
    def _():
        c_ref[...] = jnp.zeros(c_ref.shape, F32)
        m_ref[...] = jnp.full(m_ref.shape, -jnp.inf, F32)

    r_i = lax.broadcasted_iota(jnp.int32, (L, L), 0)
    c_i = lax.broadcasted_iota(jnp.int32, (L, L), 1)
    causal = c_i <= r_i
    upper = (r_i <= c_i).astype(BF16)
    ones_blk = jnp.ones((L, 128), BF16)

    def prepare(c, slot):
        row0 = pl.multiple_of(c * L, L)
        g_all = gates_ref[c]
        b_all = sum(_dot(part, upper) for part in _split3(g_all))
        for h in range(MLSTM_HEADS):
            li = g_all[h:h + 1, :]
            b_row = b_all[MLSTM_HEADS + h:MLSTM_HEADS + h + 1, :]
            q = q_ref[pl.ds(row0, L), h * dk:(h + 1) * dk]
            kt = kt_ref[c, h * dk:(h + 1) * dk, :]
            v = v_ref[pl.ds(row0, L), h * dv:(h + 1) * dv]
            v_ext = jnp.concatenate([v, ones_blk], axis=1)

            b_col = jnp.broadcast_to(b_row, (L, L)).T
            a_row = li - b_row
            d = jnp.where(causal, b_col + a_row, -jnp.inf)
            m_loc = jnp.max(d, axis=-1, keepdims=True)
            qk = _dot(q, kt) * jnp.exp(d - m_loc)
            ext_ref[slot, h] = _dot(qk.astype(BF16), v)
            den_ref[slot, h] = jnp.sum(qk, axis=-1, keepdims=True)
            mloc_ref[slot, h] = m_loc
            bcol_ref[slot, h] = b_col[:, 0:1]

            b_last = b_row[:, L - 1:L]
            g_row = b_last + a_row
            mg = jnp.max(g_row, axis=-1, keepdims=True)
            kts = (kt.astype(F32) * jnp.exp(g_row - mg)).astype(BF16)
            u_ref[slot, h] = _dot(kts, v_ext)
            blast_ref[slot, h] = jnp.broadcast_to(b_last, (1, 128))
            mg_ref[slot, h] = jnp.broadcast_to(mg, (1, 128))

    def advance(c, slot):
        row0 = pl.multiple_of(c * L, L)
        for h in range(MLSTM_HEADS):
            m_prev = m_ref[h:h + 1, 0:1]
            q = q_ref[pl.ds(row0, L), h * dk:(h + 1) * dk]
            m_loc = mloc_ref[slot, h]
            inter = bcol_ref[slot, h] + m_prev
            m_t = jnp.maximum(inter, m_loc)
            r_loc = jnp.exp(m_loc - m_t)
            s_inter = jnp.exp(inter - m_t)
            carried = _dot(q, c_ref[h].astype(BF16))
            num = ext_ref[slot, h] * r_loc + s_inter * carried[:, :dv]
            den = den_ref[slot, h] * r_loc + s_inter * carried[:, dv:dv + 1]
            hid = num / jnp.maximum(jnp.abs(den), jnp.exp(-m_t))

            hn = _rms(hid, hn_ref[h:h + 1, :])
            og = o_ref[pl.ds(row0, L), h * dv:(h + 1) * dv]
            out_ref[pl.ds(row0, L), h * dv:(h + 1) * dv] = (jax.nn.sigmoid(og) * hn).astype(BF16)

            b_last = blast_ref[slot, h][:, 0:1]
            mg = mg_ref[slot, h][:, 0:1]
            m_new = jnp.maximum(b_last + m_prev, mg)
            c_ref[h] = jnp.exp(b_last + m_prev - m_new) * c_ref[h] + jnp.exp(mg - m_new) * u_ref[slot, h]
            m_ref[h:h + 1, :] = jnp.broadcast_to(m_new, (1, 128))

    prepare(0, 0)

    def two_chunks(i, carry):
        c = 2 * i
        prepare(c + 1, 1)
        advance(c, 0)
        prepare(c + 2, 0)
        advance(c + 1, 1)
        return carry

    lax.fori_loop(0, n_chunks // 2 - 1, two_chunks, 0)
    prepare(n_chunks - 1, 1)
    advance(n_chunks - 2, 0)
    advance(n_chunks - 1, 1)


def _mlstm_core(q, kt, v, o, gates, head_norm, batch, seq):
    t = q.shape[0]
    ns = seq // MLSTM_SEQ_TILE
    cps = MLSTM_SEQ_TILE // MLSTM_CHUNK

    def rows(width):
        return pl.BlockSpec((MLSTM_SEQ_TILE, width), lambda b, s: (b * ns + s, 0))

    def slabs(height):
        return pl.BlockSpec((cps, height, MLSTM_CHUNK), lambda b, s: (b * ns + s, 0, 0))

    assert cps % 2 == 0 and cps >= 4
    col_stat = pltpu.VMEM((2, MLSTM_HEADS, MLSTM_CHUNK, 1), F32)
    scalar_stat = pltpu.VMEM((2, MLSTM_HEADS, 1, 128), F32)

    return pl.pallas_call(
        _mlstm_core_body,
        grid=(batch, ns),
        in_specs=[rows(MLSTM_QK_W), slabs(MLSTM_QK_W), rows(D_MODEL), rows(D_MODEL), slabs(GATE_ROWS),
                  pl.BlockSpec(head_norm.shape, lambda b, s: (0, 0))],
        out_specs=rows(D_MODEL),
        out_shape=jax.ShapeDtypeStruct((t, D_MODEL), BF16),
        scratch_shapes=[pltpu.VMEM((MLSTM_HEADS, MLSTM_DK, MLSTM_DV + 128), F32),
                        pltpu.VMEM((8, 128), F32),
                        pltpu.VMEM((2, MLSTM_HEADS, MLSTM_CHUNK, MLSTM_DV), F32),
                        pltpu.VMEM((2, MLSTM_HEADS, MLSTM_DK, MLSTM_DV + 128), F32),
                        col_stat, col_stat, col_stat, scalar_stat, scalar_stat],
        compiler_params=_params("parallel", "arbitrary"),
        name="mlstm_core",
    )(q, kt, v, o, gates, head_norm)


def _attn_body(qt_ref, k_ref, vt_ref, lam_ref, subln_ref, o_ref,
               sa_ref, sb_ref, bma_ref, bmb_ref, m_ref, acc_ref, *, lam_init):
    refs = (qt_ref, k_ref, vt_ref, lam_ref, subln_ref, o_ref, sa_ref, sb_ref, bma_ref, bmb_ref, m_ref, acc_ref)
    n_tiles = qt_ref.shape[1]
    _attn_query_tile(jnp.int32(0), refs, lam_init, first=True)

    def later_tile(qi, carry):
        _attn_query_tile(qi, refs, lam_init, first=False)
        return carry

    lax.fori_loop(1, n_tiles, later_tile, 0)
    _attn_write_tile(n_tiles - 1, refs, lam_init)


def _attn_write_tile(qi, refs, lam_init):
    _, _, _, lam_ref, subln_ref, o_ref, _, _, _, _, _, acc_ref = refs
    dv = DIFF_PAIR_W
    lv = lam_ref[...]
    lam = (jnp.exp(jnp.sum(lv[0:1, :] * lv[1:2, :], axis=-1, keepdims=True))
           - jnp.exp(jnp.sum(lv[2:3, :] * lv[3:4, :], axis=-1, keepdims=True)) + lam_init)
    for head in range(ATTN_HEADS_PER_STEP):
        c1, c2 = 2 * head, 2 * head + 1
        ot = (acc_ref[c1, :dv, :] / acc_ref[c1, dv:dv + 1, :]
              - lam * (acc_ref[c2, :dv, :] / acc_ref[c2, dv:dv + 1, :]))
        ms = jnp.mean(ot * ot, axis=0, keepdims=True)
        o_ref[0, qi, head * dv:(head + 1) * dv, :] = (
            ((ot * lax.rsqrt(ms + EPS)) * subln_ref[...]) * (1.0 - lam_init)).astype(BF16)


def _attn_query_tile(qi, refs, lam_init, *, first):
    qt_ref, k_ref, vt_ref, lam_ref, subln_ref, o_ref, sa_ref, sb_ref, bma_ref, bmb_ref, m_ref, acc_ref = refs
    tq, tk = ATTN_Q_TILE, ATTN_K_TILE
    dv = DIFF_PAIR_W
    n_chains = 2 * ATTN_HEADS_PER_STEP
    if not first:
        _attn_write_tile(qi - 1, refs, lam_init)
    qt = qt_ref[0, qi]
    row = lax.broadcasted_iota(jnp.int32, qt.shape, 0)
    zero = jnp.zeros_like(qt)
    qts = [jnp.where((row >= c * DIFF_HEAD_DIM) & (row < (c + 1) * DIFF_HEAD_DIM), qt, zero)
           for c in range(n_chains)]
    acc_ref[...] = jnp.zeros(acc_ref.shape, F32)
    m_ref[...] = jnp.full(m_ref.shape, -jnp.inf, F32)

    def score(pair, half, s_ref, bm_ref, mask_offset, q0=0):
        k0 = pl.multiple_of(pair * tq + half * tk, tk)
        k = k_ref[0, pl.ds(k0, tk), :]
        for c, qm in enumerate(qts):
            s = _dot(k, qm[:, q0:])
            if mask_offset is not None:
                key = lax.broadcasted_iota(jnp.int32, s.shape, 0) + mask_offset
                qry = lax.broadcasted_iota(jnp.int32, s.shape, 1) + q0
                s = jnp.where(key <= qry, s, -jnp.inf)
            s_ref[c, :, q0:] = s
            bm_ref[c, :, q0:] = jnp.max(s, axis=0, keepdims=True)

    ones_rows = jnp.ones((ATTN_SUM_ROWS, tk), BF16)

    def absorb(pair, half, s_ref, bm_ref, q0=0):
        for c in range(n_chains):
            head = c // 2
            vt = vt_ref[0, pair, head * dv:(head + 1) * dv, half * tk:(half + 1) * tk]
            vt_ext = jnp.concatenate([vt, ones_rows], axis=0)
            m = m_ref[c, :, q0:]
            m_new = jnp.maximum(m, bm_ref[c, :, q0:])
            alpha = jnp.exp2(m - m_new)
            p = jnp.exp2(s_ref[c, :, q0:] - m_new)
            acc_ref[c, :, q0:] = alpha * acc_ref[c, :, q0:] + _dot(vt_ext, p.astype(BF16))
            m_ref[c, :, q0:] = m_new

    def pair_step(pair, next_is_diagonal):
        score(pair, 1, sb_ref, bmb_ref, None)
        absorb(pair, 0, sa_ref, bma_ref)
        score(pair + 1, 0, sa_ref, bma_ref, 0 if next_is_diagonal else None)
        absorb(pair, 1, sb_ref, bmb_ref)

    def diagonal_tail():
        score(qi, 1, sb_ref, bmb_ref, tk, q0=tk)
        absorb(qi, 0, sa_ref, bma_ref)
        absorb(qi, 1, sb_ref, bmb_ref, q0=tk)

    if first:
        score(0, 0, sa_ref, bma_ref, 0)
        diagonal_tail()
        return

    score(0, 0, sa_ref, bma_ref, None)
    n_plain = qi - 1

    def body(i, carry):
        pair_step(2 * i, False)
        pair_step(2 * i + 1, False)
        return carry

    lax.fori_loop(0, lax.shift_right_logical(n_plain, 1), body, 0)

    @pl.when((n_plain & 1) == 1)
    def _():
        pair_step(n_plain - 1, False)

    pair_step(qi - 1, True)
    diagonal_tail()


def _diff_attn(qt, k, vt, lam_vecs, subln, lam_init):
    assert ATTN_Q_TILE == TOKEN_TILE and ATTN_Q_TILE == 2 * ATTN_K_TILE
    batch, seq, _ = k.shape
    nqb = seq // ATTN_Q_TILE
    n_chains = 2 * ATTN_HEADS_PER_STEP
    width = ATTN_HEADS_PER_STEP * DIFF_PAIR_W
    stat = pltpu.VMEM((n_chains, 1, ATTN_Q_TILE), F32)
    scores = pltpu.VMEM((n_chains, ATTN_K_TILE, ATTN_Q_TILE), F32)
    return pl.pallas_call(
        functools.partial(_attn_body, lam_init=lam_init),
        grid=(batch, DIFF_HEADS // ATTN_HEADS_PER_STEP),
        in_specs=[
            pl.BlockSpec((1, nqb, width, ATTN_Q_TILE), lambda b, h: (b, 0, h, 0)),
            pl.BlockSpec((1, seq, width), lambda b, h: (b, 0, h)),
            pl.BlockSpec((1, nqb, width, ATTN_Q_TILE), lambda b, h: (b, 0, h, 0)),
            pl.BlockSpec(lam_vecs.shape, lambda b, h: (0, 0)),
            pl.BlockSpec(subln.shape, lambda b, h: (0, 0)),
        ],
        out_specs=pl.BlockSpec((1, nqb, width, ATTN_Q_TILE), lambda b, h: (b, 0, h, 0)),
        out_shape=jax.ShapeDtypeStruct((batch, nqb, D_MODEL, ATTN_Q_TILE), BF16),
        scratch_shapes=[scores, scores, stat, stat, stat,
                        pltpu.VMEM((n_chains, DIFF_PAIR_W + ATTN_SUM_ROWS, ATTN_Q_TILE), F32)],
        compiler_params=_params("parallel", "parallel"),
        name="diff_attn",
    )(qt, k, vt, lam_vecs, subln)


def kernel(x, p, norm_g, w_ffn_in, w_ffn_out, w_ple_proj, w_ple_gate, mlstm_w_in, mlstm_b_gates,
           mlstm_head_norm, mlstm_w_out, kv_norm, w_kv, diff_w_q, diff_lambda, diff_subln, diff_w_out):
    batch, seq, _ = x.shape
    depth = norm_g.shape[0]
    n_a = depth // 2
    t = batch * seq
    assert t % TOKEN_TILE == 0 and seq % MLSTM_SEQ_TILE == 0 and seq % ATTN_Q_TILE == 0
    assert TOKEN_TILE % MLSTM_CHUNK == 0 and MLSTM_SEQ_TILE % TOKEN_TILE == 0

    bf = lambda w: w.astype(BF16)
    x = x.reshape(t, D_MODEL)
    p = p.reshape(depth, t, PLE_DIM)
    w_ffn_in, w_ffn_out, w_ple_gate, w_ple_proj = bf(w_ffn_in), bf(w_ffn_out), bf(w_ple_gate), bf(w_ple_proj)
    k_sh = vt_sh = None
    for layer in range(depth):
        g = (norm_g, (layer,))
        ffn1 = ((w_ffn_in, (layer, 0)), (w_ffn_out, (layer, 0)))
        ffn2 = ((w_ffn_in, (layer, 1)), (w_ffn_out, (layer, 1)))
        ple = ((w_ple_gate, (layer,)), (w_ple_proj, (layer,)))
        if layer < n_a:
            w_in = mlstm_w_in[layer]
            q_end, k_end = MLSTM_QK_W, 2 * MLSTM_QK_W
            v_end, o_end = k_end + D_MODEL, k_end + 2 * D_MODEL
            pad = GATE_ROWS - 2 * MLSTM_HEADS
            wgt = jnp.pad(w_in[:, o_end:].T, ((0, pad), (0, 0)))
            bg = jnp.pad(mlstm_b_gates[layer], (0, pad)).reshape(GATE_ROWS, 1)
            wkgt = jnp.concatenate([w_in[:, q_end:k_end].T, wgt], axis=0)
            proj = (bf(w_in[:, :q_end]), bf(wkgt), bf(w_in[:, k_end:v_end]), bf(w_in[:, v_end:o_end]), bg)
            x, q, kt, v, o, gates = _pre_mixer(x, g, *ffn1, proj, "mlstm", batch)
            h = _mlstm_core(q, kt, v, o, gates, mlstm_head_norm[layer], batch, seq)
            w_mix = bf(mlstm_w_out[layer])
        else:
            j = layer - n_a
            lam_init = 0.8 - 0.6 * math.exp(-0.3 * layer)
            x, qt = _pre_mixer(x, g, *ffn1, (bf(diff_w_q[j].T),), "attn", batch)
            h = _diff_attn(qt, k_sh.reshape(batch, seq, D_MODEL), vt_sh,
                           diff_lambda[j], diff_subln[j].reshape(DIFF_PAIR_W, 1), lam_init)
            w_mix = bf(diff_w_out[j])
        if layer == n_a - 1:
            x, k_sh, vt_sh = _post_mixer(x, h, (p, layer), g, w_mix, *ffn2, *ple,
                                         kv_norm.reshape(1, D_MODEL), bf(w_kv[:, :D_MODEL]),
                                         bf(w_kv[:, D_MODEL:].T), batch)
        else:
            x = _post_mixer(x, h, (p, layer), g, w_mix, *ffn2, *ple)
    return x.reshape(batch, seq, D_MODEL)
```
